```python
import math
import jax, jax.numpy as jnp
from jax import lax
import numpy as np

D_MODEL = 1024
BATCH = 1
SEQ = 16384
DEPTH = 1
DEC_BATCH = 8
DEC_SEQ = 2048
PAST_LEN = 128

GRID_W = 64
Q_BLOCK = 128
ROPE_THETA = 10000.0
EPS = 1e-6
N_HEADS_A = 8
N_KV_A = 2
GROUP_A = N_HEADS_A // N_KV_A
HEAD_DIM_A = 64
N_HEADS_B = 8
Q_LORA = 768
KV_LORA = 256
NOPE_DIM = 64
ROPE_DIM_B = 32
V_DIM_B = 64
N_EXPERTS = 32
TOP_K = 4
D_FF = D_MODEL
SWIGLU_ALPHA = 1.702
SWIGLU_LIMIT = 7.0
EXPERT_BLOCK = 256
IN_SIZES = (N_HEADS_A * HEAD_DIM_A, N_KV_A * HEAD_DIM_A, N_KV_A * HEAD_DIM_A,
            Q_LORA, KV_LORA, ROPE_DIM_B, 2 * D_MODEL)
IN_COLS = sum(IN_SIZES)
WIDTH_A = N_HEADS_A * HEAD_DIM_A
WIDTH_B = N_HEADS_B * V_DIM_B

kernel_name = 'hybrid_gqa_mla_moe_encoder'


def rmsnorm(x, g):
    xf = x.astype(jnp.float32)
    y = xf * lax.rsqrt(jnp.mean(xf * xf, axis=-1, keepdims=True) + EPS)
    return (y * g.astype(jnp.float32)).astype(x.dtype)


def axial_angles(n_tok, rot_dim):
    rows = n_tok // GRID_W
    row = jnp.repeat(jnp.arange(rows, dtype=jnp.float32), GRID_W)
    col = jnp.tile(jnp.arange(GRID_W, dtype=jnp.float32), rows)
    half = rot_dim // 2
    inv = ROPE_THETA ** (-jnp.arange(0, half, 2, dtype=jnp.float32) / half)
    return row[:, None] * inv, col[:, None] * inv


def _rotate_half(x, ang):
    c = jnp.cos(ang)[None, :, None, :].astype(x.dtype)
    s = jnp.sin(ang)[None, :, None, :].astype(x.dtype)
    x1, x2 = jnp.split(x, 2, axis=-1)
    return jnp.concatenate([x1 * c - x2 * s, x2 * c + x1 * s], axis=-1)


def apply_axial_rope(x, ang_row, ang_col):
    h = x.shape[-1] // 2
    return jnp.concatenate([_rotate_half(x[..., :h], ang_row),
                            _rotate_half(x[..., h:], ang_col)], axis=-1)


def block_attention(q, k, v, scale):
    b, s, hk, g, dk = q.shape
    dv = v.shape[-1]
    nb = s // Q_BLOCK
    qb = q.reshape(b, nb, Q_BLOCK, hk, g, dk).transpose(1, 0, 2, 3, 4, 5)

    def one_block(qi):
        sc = jnp.einsum('bqhgd,bkhd->bhgqk', qi, k, preferred_element_type=jnp.float32) * scale
        p = jax.nn.softmax(sc, axis=-1).astype(v.dtype)
        return jnp.einsum('bhgqk,bkhd->bqhgd', p, v)

    o = lax.map(one_block, qb)
    return o.transpose(1, 0, 2, 3, 4, 5).reshape(b, s, hk * g * dv)


def moe_ffn(x, w_router, b_router, w_gu, b_gu, w_down, b_down):
    n_tok = x.shape[0]
    n_assign = n_tok * TOP_K
    n_blocks = -(-n_assign // EXPERT_BLOCK) + N_EXPERTS
    cap = n_blocks * EXPERT_BLOCK
    logits = jnp.einsum('nd,de->ne', x, w_router, preferred_element_type=jnp.float32) + b_router.astype(jnp.float32)
    top_v, top_i = lax.top_k(logits, TOP_K)
    gate = jax.nn.softmax(top_v, axis=-1).astype(x.dtype)
    flat_e = top_i.reshape(-1)
    flat_tok = jnp.arange(n_assign, dtype=jnp.int32) // TOP_K
    order = jnp.argsort(flat_e)
    sorted_e = flat_e[order]
    counts = jnp.bincount(flat_e, length=N_EXPERTS)
    padded = (counts + EXPERT_BLOCK - 1) // EXPERT_BLOCK * EXPERT_BLOCK
    pad_end = jnp.cumsum(padded)
    start = jnp.cumsum(counts) - counts
    dest = pad_end[sorted_e] - padded[sorted_e] + jnp.arange(n_assign, dtype=jnp.int32) - start[sorted_e]
    slot_tok = jnp.zeros((cap,), jnp.int32).at[dest].set(flat_tok[order])
    slot_gate = jnp.zeros((cap,), x.dtype).at[dest].set(gate.reshape(-1)[order])
    block_e = jnp.minimum(jnp.searchsorted(pad_end, jnp.arange(n_blocks, dtype=jnp.int32) * EXPERT_BLOCK, side='right'),
                          N_EXPERTS - 1)
    xb = x[slot_tok].reshape(n_blocks, EXPERT_BLOCK, x.shape[-1])

    def expert_block(args):
        xe, e = args
        h = xe @ w_gu[e] + b_gu[e]
        hg, hl = jnp.split(h, 2, axis=-1)
        hg = jnp.minimum(hg, SWIGLU_LIMIT)
        hl = jnp.clip(hl, -SWIGLU_LIMIT, SWIGLU_LIMIT)
        a = hg * jax.nn.sigmoid(SWIGLU_ALPHA * hg) * (hl + 1.0)
        return a @ w_down[e] + b_down[e]

    yb = lax.map(expert_block, (xb, block_e))
    return jax.ops.segment_sum(yb.reshape(cap, -1) * slot_gate[:, None], slot_tok, num_segments=n_tok)


def encoder_layer(x, c, ang_a, ang_b, w_mod, b_mod, g_attn, g_moe, w_in, g_qn, g_kn,
                  g_qlat, w_uq, g_kvlat, w_ukv, w_br_a, w_br_b, w_out,
                  w_router, b_router, w_gu, b_gu, w_down, b_down):
    b, s, d = x.shape
    mod = jax.nn.silu(c) @ w_mod + b_mod
    sh_a, sc_a, gt_a, sh_m, sc_m, gt_m = jnp.split(mod[:, None, :], 6, axis=-1)
    u = rmsnorm(x, g_attn) * (1.0 + sc_a) + sh_a
    proj = u @ w_in
    cuts = np.cumsum(IN_SIZES)[:-1].tolist()
    q_a, k_a, v_a, q_lat, kv_lat, k_pe, gates = jnp.split(proj, cuts, axis=-1)

    q_a = apply_axial_rope(rmsnorm(q_a.reshape(b, s, N_HEADS_A, HEAD_DIM_A), g_qn), *ang_a)
    k_a = apply_axial_rope(rmsnorm(k_a.reshape(b, s, N_KV_A, HEAD_DIM_A), g_kn), *ang_a)
    v_a = v_a.reshape(b, s, N_KV_A, HEAD_DIM_A)
    q_a = q_a.reshape(b, s, N_KV_A, GROUP_A, HEAD_DIM_A)
    o_a = block_attention(q_a, k_a, v_a, HEAD_DIM_A ** -0.5)

    q_b = (rmsnorm(q_lat, g_qlat) @ w_uq).reshape(b, s, N_HEADS_B, NOPE_DIM + ROPE_DIM_B)
    q_nope, q_pe = jnp.split(q_b, [NOPE_DIM], axis=-1)
    q_pe = apply_axial_rope(q_pe, *ang_b)
    kv = (rmsnorm(kv_lat, g_kvlat) @ w_ukv).reshape(b, s, N_HEADS_B, NOPE_DIM + V_DIM_B)
    k_nope, v_b = jnp.split(kv, [NOPE_DIM], axis=-1)
    k_pe = apply_axial_rope(k_pe.reshape(b, s, 1, ROPE_DIM_B), *ang_b)
    q_b = jnp.concatenate([q_nope, q_pe], axis=-1)[:, :, :, None, :]
    k_b = jnp.concatenate([k_nope, jnp.broadcast_to(k_pe, (b, s, N_HEADS_B, ROPE_DIM_B))], axis=-1)
    o_b = block_attention(q_b, k_b, v_b, (NOPE_DIM + ROPE_DIM_B) ** -0.5)

    gate_a, gate_b = jnp.split(gates, 2, axis=-1)
    mixed = jax.nn.sigmoid(gate_a) * (o_a @ w_br_a) + jax.nn.sigmoid(gate_b) * (o_b @ w_br_b)
    x = x + gt_a * (mixed @ w_out)

    u = rmsnorm(x, g_moe) * (1.0 + sc_m) + sh_m
    y = moe_ffn(u.reshape(b * s, d), w_router, b_router, w_gu, b_gu, w_down, b_down).reshape(b, s, d)
    return x + gt_m * y


def run_trunk(x, c, w_mod, b_mod, g_attn, g_moe, w_in, g_qn, g_kn, g_qlat, w_uq, g_kvlat, w_ukv,
              w_br_a, w_br_b, w_out, w_router, b_router, w_gu, b_gu, w_down, b_down, g_final):
    n_tok = x.shape[1]
    ang_a = axial_angles(n_tok, HEAD_DIM_A)
    ang_b = axial_angles(n_tok, ROPE_DIM_B)
    for l in range(DEPTH):
        x = encoder_layer(x, c, ang_a, ang_b, w_mod[l], b_mod[l], g_attn[l], g_moe[l], w_in[l],
                          g_qn[l], g_kn[l], g_qlat[l], w_uq[l], g_kvlat[l], w_ukv[l],
                          w_br_a[l], w_br_b[l], w_out[l], w_router[l], b_router[l],
                          w_gu[l], b_gu[l], w_down[l], b_down[l])
    return rmsnorm(x, g_final)


def setup_inputs(seed: int = 0) -> dict:
    key = jax.random.key(seed)
    ks = jax.random.split(key, 32)
    f32 = jnp.float32

    def nrm(k, shape, scale):
        return jax.random.normal(k, shape, f32) * scale

    def gain(k, shape):
        return 1.0 + 0.05 * jax.random.normal(k, shape, f32)

    L, D, E, F = DEPTH, D_MODEL, N_EXPERTS, D_FF
    return {
        'x_prompt': nrm(ks[0], (BATCH, SEQ, D), 1.0),
        'x_sample': nrm(ks[1], (DEC_BATCH, DEC_SEQ, D), 1.0),
        'c_prompt': nrm(ks[2], (BATCH, D), 1.0),
        'c_sample': nrm(ks[3], (DEC_BATCH, D), 1.0),
        'w_mod': nrm(ks[4], (L, D, 6 * D), D ** -0.5),
        'b_mod': nrm(ks[5], (L, 6 * D), 0.01),
        'g_attn': gain(ks[6], (L, D)),
        'g_moe': gain(ks[7], (L, D)),
        'w_in': nrm(ks[8], (L, D, IN_COLS), D ** -0.5),
        'g_qn': gain(ks[9], (L, HEAD_DIM_A)),
        'g_kn': gain(ks[10], (L, HEAD_DIM_A)),
        'g_qlat': gain(ks[11], (L, Q_LORA)),
        'w_uq': nrm(ks[12], (L, Q_LORA, N_HEADS_B * (NOPE_DIM + ROPE_DIM_B)), Q_LORA ** -0.5),
        'g_kvlat': gain(ks[13], (L, KV_LORA)),
        'w_ukv': nrm(ks[14], (L, KV_LORA, N_HEADS_B * (NOPE_DIM + V_DIM_B)), KV_LORA ** -0.5),
        'w_br_a': nrm(ks[15], (L, WIDTH_A, D), WIDTH_A ** -0.5),
        'w_br_b': nrm(ks[16], (L, WIDTH_B, D), WIDTH_B ** -0.5),
        'w_out': nrm(ks[17], (L, D, D), D ** -0.5),
        'w_router': nrm(ks[18], (L, D, E), D ** -0.5),
        'b_router': nrm(ks[19], (L, E), 0.01),
        'w_gu': nrm(ks[20], (L, E, D, 2 * F), D ** -0.5),
        'b_gu': nrm(ks[21], (L, E, 2 * F), 0.01),
        'w_down': nrm(ks[22], (L, E, F, D), F ** -0.5),
        'b_down': nrm(ks[23], (L, E, D), 0.01),
        'g_final': gain(ks[24], (D,)),
    }


def reference(x_prompt, x_sample, c_prompt, c_sample, w_mod, b_mod, g_attn, g_moe, w_in, g_qn, g_kn,
              g_qlat, w_uq, g_kvlat, w_ukv, w_br_a, w_br_b, w_out, w_router, b_router,
              w_gu, b_gu, w_down, b_down, g_final):
    y_prompt = run_trunk(x_prompt, c_prompt, w_mod, b_mod, g_attn, g_moe, w_in, g_qn, g_kn, g_qlat, w_uq,
                         g_kvlat, w_ukv, w_br_a, w_br_b, w_out, w_router, b_router, w_gu, b_gu,
                         w_down, b_down, g_final)
    y_sample = run_trunk(x_sample, c_sample, w_mod, b_mod, g_attn, g_moe, w_in, g_qn, g_kn, g_qlat, w_uq,
                         g_kvlat, w_ukv, w_br_a, w_br_b, w_out, w_router, b_router, w_gu, b_gu,
                         w_down, b_down, g_final)
    return (y_prompt, y_sample)
```

```python
import functools

import jax
import jax.numpy as jnp
from jax import lax
from jax.experimental import pallas as pl
from jax.experimental.pallas import tpu as pltpu

F32 = jnp.float32
BF16 = jnp.bfloat16

D_MODEL = 1024
GRID_W = 64
ROPE_THETA = 10000.0
EPS = 1e-6
N_HEADS_A = 8
N_KV_A = 2
GROUP_A = N_HEADS_A // N_KV_A
HEAD_DIM_A = 64
N_HEADS_B = 8
Q_LORA = 768
KV_LORA = 256
NOPE_DIM = 64
ROPE_DIM_B = 32
V_DIM_B = 64
N_EXPERTS = 32
TOP_K = 4
D_FF = D_MODEL
SWIGLU_ALPHA = 1.702
SWIGLU_LIMIT = 7.0
EXPERT_BLOCK = 256

LANE = 128
NEG_BIG = -1e30
VMEM_LIMIT = 52 * 1024 * 1024

_C_QA = 0
_C_KA = _C_QA + N_HEADS_A * LANE
_C_QLAT = _C_KA + N_KV_A * LANE
_C_KVLAT = _C_QLAT + Q_LORA
_C_KPE = _C_KVLAT + KV_LORA
_C_GATE = _C_KPE + LANE
_C_END = _C_GATE + 2 * D_MODEL

_SH_A, _SC_A, _GT_A, _SH_M, _SC_M, _GT_M = range(6)
MOD_ROWS = 8

_NT = (((1,), (1,)), ((), ()))
_TN = (((0,), (0,)), ((), ()))


def _cparams(*sem):
    return pltpu.CompilerParams(dimension_semantics=sem, vmem_limit_bytes=VMEM_LIMIT)


def _rms(x, g):
    return x * lax.rsqrt(jnp.mean(x * x, axis=-1, keepdims=True) + EPS) * g


def _mod_kernel(c_ref, w_ref, b_ref, o_ref):
    c = c_ref[...]
    s = c * jax.nn.sigmoid(c)
    o_ref[...] = jnp.dot(s.astype(BF16), w_ref[...].astype(BF16), preferred_element_type=F32) + b_ref[...]


def _modulation(c, w_mod, b_mod):
    nb, d = c.shape
    rows = -(-nb // 8) * 8
    cp = jnp.zeros((rows, d), F32).at[:nb].set(c)
    n_out = w_mod.shape[1]
    tn = 512
    out = pl.pallas_call(
        _mod_kernel,
        grid=(n_out // tn,),
        in_specs=[pl.BlockSpec((rows, d), lambda j: (0, 0)),
                  pl.BlockSpec((d, tn), lambda j: (0, j)),
                  pl.BlockSpec((1, tn), lambda j: (0, j))],
        out_specs=pl.BlockSpec((rows, tn), lambda j: (0, j)),
        out_shape=jax.ShapeDtypeStruct((rows, n_out), F32),
        compiler_params=_cparams("arbitrary"),
        name="mod",
    )(cp, w_mod, b_mod.reshape(1, n_out))
    mod = out[:nb].reshape(nb, 6, d)
    return jnp.concatenate([mod, jnp.zeros((nb, MOD_ROWS - 6, d), F32)], axis=1)


def _rope(x, cos, sinm, sinp, shift):
    return x * cos + pltpu.roll(x, LANE - shift, 1) * sinm + pltpu.roll(x, shift, 1) * sinp


def _pre_kernel(x_ref, mod_ref, gattn_ref, w1_ref, wvat_ref, gqa_ref, gka_ref, gqlat_ref, wuq_ref,
                gkvlat_ref, wuk_ref, wuvt_ref, ta_ref, tb_ref,
                qa_ref, ka_ref, vta_ref, qb_ref, kb_ref, vtb_ref, sig_ref):
    x = x_ref[...]
    u = _rms(x, gattn_ref[...]) * (1.0 + mod_ref[_SC_A:_SC_A + 1, :]) + mod_ref[_SH_A:_SH_A + 1, :]
    ub = u.astype(BF16)

    def proj(lo, hi):
        return jnp.dot(ub, w1_ref[:, lo:hi], preferred_element_type=F32)

    cos_a, sinm_a, sinp_a = ta_ref[0], ta_ref[1], ta_ref[2]
    cos_b, sinm_b, sinp_b = tb_ref[0], tb_ref[1], tb_ref[2]
    shift_a = HEAD_DIM_A // 4
    shift_b = ROPE_DIM_B // 4

    def head_a(xs, g, scale):
        ms = jnp.sum(xs * xs, axis=-1, keepdims=True) * (1.0 / HEAD_DIM_A)
        xn = xs * lax.rsqrt(ms + EPS) * g
        return (_rope(xn, cos_a, sinm_a, sinp_a, shift_a) * scale).astype(BF16)

    qa = proj(_C_QA, _C_KA)
    for h in range(N_HEADS_A):
        qa_ref[:, h * LANE:(h + 1) * LANE] = head_a(qa[:, h * LANE:(h + 1) * LANE], gqa_ref[...],
                                                    HEAD_DIM_A ** -0.5)
    ka = proj(_C_KA, _C_QLAT)
    for h in range(N_KV_A):
        ka_ref[:, h * LANE:(h + 1) * LANE] = head_a(ka[:, h * LANE:(h + 1) * LANE], gka_ref[...], 1.0)
    vta = lax.dot_general(wvat_ref[...], ub, _NT, preferred_element_type=F32)
    vta_ref[0] = vta.astype(BF16).reshape(N_KV_A, HEAD_DIM_A, vta.shape[-1])

    qn = _rms(proj(_C_QLAT, _C_KVLAT), gqlat_ref[...]).astype(BF16)
    qb = jnp.dot(qn, wuq_ref[...], preferred_element_type=F32)
    scale_b = (NOPE_DIM + ROPE_DIM_B) ** -0.5
    for h in range(N_HEADS_B):
        sl = slice(h * LANE, (h + 1) * LANE)
        qb_ref[:, sl] = (_rope(qb[:, sl], cos_b, sinm_b, sinp_b, shift_b) * scale_b).astype(BF16)
    kvn = _rms(proj(_C_KVLAT, _C_KPE), gkvlat_ref[...]).astype(BF16)
    kpe = _rope(proj(_C_KPE, _C_GATE), cos_b, sinm_b, sinp_b, shift_b)
    kb = jnp.dot(kvn, wuk_ref[...], preferred_element_type=F32)
    for h in range(N_HEADS_B):
        sl = slice(h * LANE, (h + 1) * LANE)
        kb_ref[:, sl] = (kb[:, sl] + kpe).astype(BF16)
    vtb = lax.dot_general(wuvt_ref[...], kvn, _NT, preferred_element_type=F32)
    vtb_ref[0] = vtb.astype(BF16).reshape(N_HEADS_B, V_DIM_B, vtb.shape[-1])

    sig_ref[...] = jax.nn.sigmoid(proj(_C_GATE, _C_END)).astype(BF16)


def _pre(x2d, mod, wp, tables_a, tables_b, seq, tm):
    n_tok, d = x2d.shape
    n_tiles = n_tok // tm
    tiles_per_seq = seq // tm
    full = lambda a: pl.BlockSpec(a.shape, lambda i: (0,) * a.ndim)
    tok = lambda cols: pl.BlockSpec((tm, cols), lambda i: (i, 0))
    tab = pl.BlockSpec((3, tm, LANE), lambda i: (0, i % tiles_per_seq, 0))
    in_specs = [tok(d),
                pl.BlockSpec((None, MOD_ROWS, d), lambda i: (i // tiles_per_seq, 0, 0)),
                full(wp["g_attn"]), full(wp["w1"]), full(wp["wvat"]), full(wp["g_qa"]), full(wp["g_ka"]),
                full(wp["g_qlat"]), full(wp["wuq"]), full(wp["g_kvlat"]), full(wp["wuk"]), full(wp["wuvt"]),
                tab, tab]
    out_shape = [
        jax.ShapeDtypeStruct((n_tok, N_HEADS_A * LANE), BF16),
        jax.ShapeDtypeStruct((n_tok, N_KV_A * LANE), BF16),
        jax.ShapeDtypeStruct((n_tiles, N_KV_A, HEAD_DIM_A, tm), BF16),
        jax.ShapeDtypeStruct((n_tok, N_HEADS_B * LANE), BF16),
        jax.ShapeDtypeStruct((n_tok, N_HEADS_B * LANE), BF16),
        jax.ShapeDtypeStruct((n_tiles, N_HEADS_B, V_DIM_B, tm), BF16),
        jax.ShapeDtypeStruct((n_tok, 2 * D_MODEL), BF16),
    ]
    out_specs = [tok(N_HEADS_A * LANE), tok(N_KV_A * LANE),
                 pl.BlockSpec((1, N_KV_A, HEAD_DIM_A, tm), lambda i: (i, 0, 0, 0)),
                 tok(N_HEADS_B * LANE), tok(N_HEADS_B * LANE),
                 pl.BlockSpec((1, N_HEADS_B, V_DIM_B, tm), lambda i: (i, 0, 0, 0)),
                 tok(2 * D_MODEL)]
    return pl.pallas_call(
        _pre_kernel, grid=(n_tiles,), in_specs=in_specs, out_specs=out_specs, out_shape=out_shape,
        compiler_params=_cparams("arbitrary"), name="pre",
    )(x2d, mod, wp["g_attn"], wp["w1"], wp["wvat"], wp["g_qa"], wp["g_ka"], wp["g_qlat"], wp["wuq"],
      wp["g_kvlat"], wp["wuk"], wp["wuvt"], tables_a, tables_b)


def _attn_kernel(q_ref, k_ref, vt_ref, o_ref, *, n_sub, tq, tk, n_chunks, dv):
    if n_sub == 1:
        q = q_ref[...]
    else:
        q = jnp.concatenate([q_ref[:, j * LANE:(j + 1) * LANE] for j in range(n_sub)], axis=0)
    cols = n_sub * tq

    def body(i, carry):
        m, l, acc = carry
        k = k_ref[pl.ds(pl.multiple_of(i * tk, tk), tk), :]
        s = lax.dot_general(k, q, _NT, preferred_element_type=F32)
        m_new = jnp.maximum(m, jnp.max(s, axis=0, keepdims=True))
        alpha = jnp.exp(m - m_new)
        p = jnp.exp(s - m_new)
        l = alpha * l + jnp.sum(p, axis=0, keepdims=True)
        acc = alpha * acc + jnp.dot(vt_ref[i], p.astype(BF16), preferred_element_type=F32)
        return m_new, l, acc

    init = (jnp.full((1, cols), NEG_BIG, F32), jnp.zeros((1, cols), F32), jnp.zeros((dv, cols), F32))
    _, l, acc = lax.fori_loop(0, n_chunks, body, init)
    o = acc / l
    for j in range(n_sub):
        o_ref[j * dv:(j + 1) * dv, :] = o[:, j * tq:(j + 1) * tq].astype(BF16)


def _attention(q, k, vt, *, n_batch, seq, n_groups, n_sub, tq, dv):
    n_tok = q.shape[0]
    tk = vt.shape[-1]
    n_chunks = seq // tk
    q_tiles = seq // tq
    kern = functools.partial(_attn_kernel, n_sub=n_sub, tq=tq, tk=tk, n_chunks=n_chunks, dv=dv)
    return pl.pallas_call(
        kern,
        grid=(n_batch, n_groups, q_tiles),
        in_specs=[pl.BlockSpec((tq, n_sub * LANE), lambda b, g, i: (b * q_tiles + i, g)),
                  pl.BlockSpec((seq, LANE), lambda b, g, i: (b, g)),
                  pl.BlockSpec((n_chunks, None, dv, tk), lambda b, g, i: (b, g, 0, 0))],
        out_specs=pl.BlockSpec((n_sub * dv, tq), lambda b, g, i: (g, b * q_tiles + i)),
        out_shape=jax.ShapeDtypeStruct((n_groups * n_sub * dv, n_tok), BF16),
        compiler_params=_cparams("arbitrary", "arbitrary", "arbitrary"),
        name="attn",
    )(q, k, vt)


def _post_kernel(x_ref, mod_ref, ota_ref, otb_ref, sig_ref, wbra_ref, wbrb_ref, wout_ref, gmoe_ref,
                 wr_ref, br_ref,
                 x1_ref, u2_ref, eid_ref, rank_ref, gate_ref, cnt_ref, carry_ref):
    i = pl.program_id(0)

    @pl.when(i == 0)
    def _():
        carry_ref[...] = jnp.zeros_like(carry_ref)

    d = D_MODEL
    ya = lax.dot_general(ota_ref[...], wbra_ref[...], _TN, preferred_element_type=F32)
    yb = lax.dot_general(otb_ref[...], wbrb_ref[...], _TN, preferred_element_type=F32)
    mixed = sig_ref[:, :d].astype(F32) * ya + sig_ref[:, d:].astype(F32) * yb
    att = jnp.dot(mixed.astype(BF16), wout_ref[...], preferred_element_type=F32)
    x1 = x_ref[...] + mod_ref[_GT_A:_GT_A + 1, :] * att
    x1_ref[...] = x1
    u2 = _rms(x1, gmoe_ref[...]) * (1.0 + mod_ref[_SC_M:_SC_M + 1, :]) + mod_ref[_SH_M:_SH_M + 1, :]
    u2_ref[...] = u2

    logits = jnp.dot(u2.astype(BF16), wr_ref[...], preferred_element_type=F32) + br_ref[...]
    tm = logits.shape[0]
    lane = lax.broadcasted_iota(jnp.int32, (tm, LANE), 1)
    vals = logits
    top_v, top_i = [], []
    for _ in range(TOP_K):
        mx = jnp.max(vals, axis=-1, keepdims=True)
        idx = jnp.min(jnp.where(vals == mx, lane, LANE), axis=-1, keepdims=True)
        top_v.append(mx)
        top_i.append(idx)
        vals = jnp.where(lane == idx, -jnp.inf, vals)
    ex = [jnp.exp(v - top_v[0]) for v in top_v]
    den = ex[0] + ex[1] + ex[2] + ex[3]

    onehot = [lane == idx for idx in top_i]
    cnt = sum(oh.astype(F32) for oh in onehot)
    r_i = lax.broadcasted_iota(jnp.int32, (tm, tm), 0)
    c_i = lax.broadcasted_iota(jnp.int32, (tm, tm), 1)
    lower = (c_i < r_i).astype(BF16)
    before = jnp.dot(lower, cnt.astype(BF16), preferred_element_type=F32) + carry_ref[...]
    eid = jnp.zeros((tm, LANE), jnp.int32)
    rank = jnp.zeros((tm, LANE), jnp.int32)
    gate = jnp.zeros((tm, LANE), F32)
    for k in range(TOP_K):
        rk = jnp.sum(jnp.where(onehot[k], before, 0.0), axis=-1, keepdims=True).astype(jnp.int32)
        eid = jnp.where(lane == k, top_i[k], eid)
        rank = jnp.where(lane == k, rk, rank)
        gate = jnp.where(lane == k, ex[k] / den, gate)
    eid_ref[...] = eid
    rank_ref[...] = rank
    gate_ref[...] = gate
    carry = carry_ref[...] + jnp.sum(cnt, axis=0, keepdims=True)
    carry_ref[...] = carry
    cnt_ref[...] = carry


def _post(x2d, mod, ota, otb, sig, wp, seq, tm):
    n_tok, d = x2d.shape
    n_tiles = n_tok // tm
    tiles_per_seq = seq // tm
    full = lambda a: pl.BlockSpec(a.shape, lambda i: (0,) * a.ndim)
    tok = lambda cols: pl.BlockSpec((tm, cols), lambda i: (i, 0))
    tcol = lambda rows: pl.BlockSpec((rows, tm), lambda i: (0, i))
    in_specs = [tok(d), pl.BlockSpec((None, MOD_ROWS, d), lambda i: (i // tiles_per_seq, 0, 0)),
                tcol(ota.shape[0]), tcol(otb.shape[0]), tok(2 * d),
                full(wp["wbra"]), full(wp["wbrb"]), full(wp["wout"]), full(wp["g_moe"]),
                full(wp["wr"]), full(wp["br"])]
    out_shape = [jax.ShapeDtypeStruct((n_tok, d), F32), jax.ShapeDtypeStruct((n_tok, d), F32),
                 jax.ShapeDtypeStruct((n_tok, LANE), jnp.int32), jax.ShapeDtypeStruct((n_tok, LANE), jnp.int32),
                 jax.ShapeDtypeStruct((n_tok, LANE), F32), jax.ShapeDtypeStruct((1, LANE), F32)]
    out_specs = [tok(d), tok(d), tok(LANE), tok(LANE), tok(LANE), pl.BlockSpec((1, LANE), lambda i: (0, 0))]
    return pl.pallas_call(
        _post_kernel, grid=(n_tiles,), in_specs=in_specs, out_specs=out_specs, out_shape=out_shape,
        scratch_shapes=[pltpu.VMEM((1, LANE), F32)],
        compiler_params=_cparams("arbitrary"), name="post",
    )(x2d, mod, ota, otb, sig, wp["wbra"], wp["wbrb"], wp["wout"], wp["g_moe"], wp["wr"], wp["br"])


def _row_copy(src, dst, sem):
    return pltpu.make_async_copy(src, dst, sem)


def _dispatch_kernel(dest_ref, u_ref, xs_in_ref, xs_ref, sem, *, tm):
    del xs_in_ref

    def issue(r, c):
        for k in range(TOP_K):
            d = dest_ref[0, 0, r * TOP_K + k]
            _row_copy(u_ref.at[pl.ds(r, 1)], xs_ref.at[pl.ds(d, 1)], sem).start()
        return c

    lax.fori_loop(0, tm, issue, 0)

    def drain(r, c):
        for k in range(TOP_K):
            _row_copy(u_ref.at[pl.ds(0, 1)], xs_ref.at[pl.ds(0, 1)], sem).wait()
        return c

    lax.fori_loop(0, tm, drain, 0)


def _dispatch(u2, dest, cap, tm):
    n_tok, d = u2.shape
    n_tiles = n_tok // tm
    dest3 = dest.reshape(n_tiles, 1, tm * TOP_K)
    xs0 = jnp.zeros((cap, d), u2.dtype)
    return pl.pallas_call(
        functools.partial(_dispatch_kernel, tm=tm),
        grid=(n_tiles,),
        in_specs=[pl.BlockSpec((1, 1, tm * TOP_K), lambda i: (i, 0, 0), memory_space=pltpu.SMEM),
                  pl.BlockSpec((tm, d), lambda i: (i, 0)),
                  pl.BlockSpec(memory_space=pl.ANY)],
        out_specs=pl.BlockSpec(memory_space=pl.ANY),
        out_shape=jax.ShapeDtypeStruct((cap, d), u2.dtype),
        scratch_shapes=[pltpu.SemaphoreType.DMA(())],
        input_output_aliases={2: 0},
        compiler_params=_cparams("arbitrary"), name="dispatch",
    )(dest3, u2, xs0)


def _expert_kernel(be_ref, valid_ref, xs_ref, wgu_ref, bgu_ref, wd_ref, bd_ref, y_ref):
    b = pl.program_id(0)

    @pl.when(valid_ref[b] == 1)
    def _():
        x = xs_ref[...].astype(BF16)
        h = jnp.dot(x, wgu_ref[0], preferred_element_type=F32) + bgu_ref[0]
        hg = jnp.minimum(h[:, :D_FF], SWIGLU_LIMIT)
        hl = jnp.clip(h[:, D_FF:], -SWIGLU_LIMIT, SWIGLU_LIMIT)
        a = hg * jax.nn.sigmoid(SWIGLU_ALPHA * hg) * (hl + 1.0)
        y_ref[...] = jnp.dot(a.astype(BF16), wd_ref[0], preferred_element_type=F32) + bd_ref[0]

    @pl.when(valid_ref[b] == 0)
    def _():
        y_ref[...] = jnp.zeros_like(y_ref)


def _experts(xs, block_e, block_valid, wp):
    cap, d = xs.shape
    n_blocks = cap // EXPERT_BLOCK
    grid_spec = pltpu.PrefetchScalarGridSpec(
        num_scalar_prefetch=2,
        grid=(n_blocks,),
        in_specs=[pl.BlockSpec((EXPERT_BLOCK, d), lambda b, be, bv: (b, 0)),
                  pl.BlockSpec((1, d, 2 * D_FF), lambda b, be, bv: (be[b], 0, 0)),
                  pl.BlockSpec((1, 1, 2 * D_FF), lambda b, be, bv: (be[b], 0, 0)),
                  pl.BlockSpec((1, D_FF, d), lambda b, be, bv: (be[b], 0, 0)),
                  pl.BlockSpec((1, 1, d), lambda b, be, bv: (be[b], 0, 0))],
        out_specs=pl.BlockSpec((EXPERT_BLOCK, d), lambda b, be, bv: (b, 0)),
    )
    return pl.pallas_call(
        _expert_kernel, grid_spec=grid_spec,
        out_shape=jax.ShapeDtypeStruct((cap, d), F32),
        compiler_params=_cparams("arbitrary"), name="experts",
    )(block_e, block_valid, xs, wp["wgu"], wp["bgu"], wp["wd"], wp["bd"])


def _combine_kernel(dest_ref, x1_ref, gate_ref, mod_ref, gfin_ref, y_hbm, o_ref, buf, sem, *, tm):
    def issue(r, c):
        for k in range(TOP_K):
            d = dest_ref[0, 0, r * TOP_K + k]
            _row_copy(y_hbm.at[pl.ds(d, 1)], buf.at[k, pl.ds(r, 1)], sem).start()
        return c

    lax.fori_loop(0, tm, issue, 0)

    def drain(r, c):
        for k in range(TOP_K):
            _row_copy(y_hbm.at[pl.ds(0, 1)], buf.at[k, pl.ds(0, 1)], sem).wait()
        return c

    lax.fori_loop(0, tm, drain, 0)
    gate = gate_ref[...]
    y = gate[:, 0:1] * buf[0]
    for k in range(1, TOP_K):
        y = y + gate[:, k:k + 1] * buf[k]
    x2 = x1_ref[...] + mod_ref[_GT_M:_GT_M + 1, :] * y
    o_ref[...] = _rms(x2, gfin_ref[...])


def _combine(x1, gate, mod, g_final, yb, dest, seq, tm):
    n_tok, d = x1.shape
    n_tiles = n_tok // tm
    tiles_per_seq = seq // tm
    dest3 = dest.reshape(n_tiles, 1, tm * TOP_K)
    return pl.pallas_call(
        functools.partial(_combine_kernel, tm=tm),
        grid=(n_tiles,),
        in_specs=[pl.BlockSpec((1, 1, tm * TOP_K), lambda i: (i, 0, 0), memory_space=pltpu.SMEM),
                  pl.BlockSpec((tm, d), lambda i: (i, 0)),
                  pl.BlockSpec((tm, LANE), lambda i: (i, 0)),
                  pl.BlockSpec((None, MOD_ROWS, d), lambda i: (i // tiles_per_seq, 0, 0)),
                  pl.BlockSpec((1, d), lambda i: (0, 0)),
                  pl.BlockSpec(memory_space=pl.ANY)],
        out_specs=pl.BlockSpec((tm, d), lambda i: (i, 0)),
        out_shape=jax.ShapeDtypeStruct((n_tok, d), F32),
        scratch_shapes=[pltpu.VMEM((TOP_K, tm, d), F32), pltpu.SemaphoreType.DMA(())],
        compiler_params=_cparams("arbitrary"), name="combine",
    )(dest3, x1, gate, mod, g_final, yb)


def _slot_cols(w, n_heads, width, offset=0):
    k = w.shape[0]
    w3 = w.reshape(k, n_heads, width)
    out = jnp.zeros((k, n_heads, LANE), w.dtype).at[:, :, offset:offset + width].set(w3)
    return out.reshape(k, n_heads * LANE)


def _prep_weights(g_attn, g_moe, w_in, g_qn, g_kn, g_qlat, w_uq, g_kvlat, w_ukv, w_br_a, w_br_b, w_out,
                  w_router, b_router, w_gu, b_gu, w_down, b_down):
    d = D_MODEL
    wa = N_HEADS_A * HEAD_DIM_A
    wk = N_KV_A * HEAD_DIM_A
    c0 = 0
    w_qa = w_in[:, c0:c0 + wa]; c0 += wa
    w_ka = w_in[:, c0:c0 + wk]; c0 += wk
    w_va = w_in[:, c0:c0 + wk]; c0 += wk
    w_ql = w_in[:, c0:c0 + Q_LORA]; c0 += Q_LORA
    w_kvl = w_in[:, c0:c0 + KV_LORA]; c0 += KV_LORA
    w_kpe = w_in[:, c0:c0 + ROPE_DIM_B]; c0 += ROPE_DIM_B
    w_gate = w_in[:, c0:]
    w1 = jnp.concatenate([
        _slot_cols(w_qa, N_HEADS_A, HEAD_DIM_A), _slot_cols(w_ka, N_KV_A, HEAD_DIM_A), w_ql, w_kvl,
        _slot_cols(w_kpe, 1, ROPE_DIM_B, NOPE_DIM), w_gate], axis=1).astype(BF16)
    ukv = w_ukv.reshape(KV_LORA, N_HEADS_B, NOPE_DIM + V_DIM_B)
    w_uk = ukv[:, :, :NOPE_DIM].reshape(KV_LORA, N_HEADS_B * NOPE_DIM)
    w_uv = ukv[:, :, NOPE_DIM:].reshape(KV_LORA, N_HEADS_B * V_DIM_B)
    pad_gain = lambda g: jnp.zeros((1, LANE), F32).at[0, :g.shape[0]].set(g)
    return {
        "g_attn": g_attn.reshape(1, d), "g_moe": g_moe.reshape(1, d),
        "w1": w1, "wvat": w_va.T.astype(BF16),
        "g_qa": pad_gain(g_qn), "g_ka": pad_gain(g_kn),
        "g_qlat": g_qlat.reshape(1, Q_LORA), "g_kvlat": g_kvlat.reshape(1, KV_LORA),
        "wuq": _slot_cols(w_uq, N_HEADS_B, NOPE_DIM + ROPE_DIM_B).astype(BF16),
        "wuk": _slot_cols(w_uk, N_HEADS_B, NOPE_DIM).astype(BF16),
        "wuvt": w_uv.T.astype(BF16),
        "wbra": w_br_a.astype(BF16), "wbrb": w_br_b.astype(BF16), "wout": w_out.astype(BF16),
        "wr": jnp.zeros((d, LANE), F32).at[:, :N_EXPERTS].set(w_router).astype(BF16),
        "br": jnp.full((1, LANE), NEG_BIG, F32).at[0, :N_EXPERTS].set(b_router),
        "wgu": w_gu.astype(BF16), "bgu": b_gu.reshape(N_EXPERTS, 1, 2 * D_FF),
        "wd": w_down.astype(BF16), "bd": b_down.reshape(N_EXPERTS, 1, d),
    }


def _rope_tables(seq, rot_dim, offset):
    half = rot_dim // 2
    quarter = half // 2
    t = jnp.arange(seq, dtype=jnp.int32)
    row = (t // GRID_W).astype(F32)
    col = (t % GRID_W).astype(F32)
    inv = ROPE_THETA ** (-jnp.arange(0, half, 2, dtype=F32) / half)
    ar, ac = row[:, None] * inv, col[:, None] * inv
    z = jnp.zeros((seq, quarter), F32)
    cos = jnp.concatenate([jnp.cos(ar), jnp.cos(ar), jnp.cos(ac), jnp.cos(ac)], axis=1)
    sinm = jnp.concatenate([-jnp.sin(ar), z, -jnp.sin(ac), z], axis=1)
    sinp = jnp.concatenate([z, jnp.sin(ar), z, jnp.sin(ac)], axis=1)

    def place(a, fill):
        return jnp.full((seq, LANE), fill, F32).at[:, offset:offset + rot_dim].set(a)

    return jnp.stack([place(cos, 1.0), place(sinm, 0.0), place(sinp, 0.0)])


def _trunk(x, mod, wp, g_final):
    n_batch, seq, d = x.shape
    n_tok = n_batch * seq
    x2d = x.reshape(n_tok, d)
    tm = min(256, seq)
    ta = _rope_tables(seq, HEAD_DIM_A, 0)
    tb = _rope_tables(seq, ROPE_DIM_B, NOPE_DIM)

    qa, ka, vta, qb, kb, vtb, sig = _pre(x2d, mod, wp, ta, tb, seq, tm)
    ota = _attention(qa, ka, vta, n_batch=n_batch, seq=seq, n_groups=N_KV_A, n_sub=GROUP_A,
                     tq=min(128, seq), dv=HEAD_DIM_A)
    otb = _attention(qb, kb, vtb, n_batch=n_batch, seq=seq, n_groups=N_HEADS_B, n_sub=1,
                     tq=min(512, seq), dv=V_DIM_B)
    x1, u2, eid, rank, gate, cnt = _post(x2d, mod, ota, otb, sig, wp, seq, tm)

    n_assign = n_tok * TOP_K
    n_blocks = -(-n_assign // EXPERT_BLOCK) + N_EXPERTS
    cap = n_blocks * EXPERT_BLOCK
    counts = cnt[0, :N_EXPERTS].astype(jnp.int32)
    padded = (counts + EXPERT_BLOCK - 1) // EXPERT_BLOCK * EXPERT_BLOCK
    pad_end = jnp.cumsum(padded)
    pad_start = pad_end - padded
    dest = pad_start[eid[:, :TOP_K]] + rank[:, :TOP_K]
    block_lo = jnp.arange(n_blocks, dtype=jnp.int32) * EXPERT_BLOCK
    block_e = jnp.minimum(jnp.searchsorted(pad_end, block_lo, side="right"), N_EXPERTS - 1).astype(jnp.int32)
    block_valid = (block_lo < pad_end[-1]).astype(jnp.int32)

    xs = _dispatch(u2, dest, cap, tm)
    yb = _experts(xs, block_e, block_valid, wp)
    out = _combine(x1, gate, mod, g_final.reshape(1, d), yb, dest, seq, min(128, seq))
    return out.reshape(n_batch, seq, d)


def kernel(x_prompt, x_sample, c_prompt, c_sample, w_mod, b_mod, g_attn, g_moe, w_in, g_qn, g_kn, g_qlat,
           w_uq, g_kvlat, w_ukv, w_br_a, w_br_b, w_out, w_router, b_router, w_gu, b_gu, w_down, b_down,
           g_final):
    assert w_mod.shape[0] == 1, "single-layer trunk"
    wp = _prep_weights(g_attn[0], g_moe[0], w_in[0], g_qn[0], g_kn[0], g_qlat[0], w_uq[0], g_kvlat[0],
                       w_ukv[0], w_br_a[0], w_br_b[0], w_out[0], w_router[0], b_router[0], w_gu[0],
                       b_gu[0], w_down[0], b_down[0])
    nb_p = c_prompt.shape[0]
    mod = _modulation(jnp.concatenate([c_prompt, c_sample], axis=0), w_mod[0], b_mod[0])
    y_prompt = _trunk(x_prompt, mod[:nb_p], wp, g_final)
    y_sample = _trunk(x_sample, mod[nb_p:], wp, g_final)
    return (y_prompt, y_sample)
```

```python
import functools

import jax
import jax.numpy as jnp
from jax import lax
from jax.experimental import pallas as pl
from jax.experimental.pallas import tpu as pltpu

F32 = jnp.float32
BF16 = jnp.bfloat16

D_MODEL = 1024
GRID_W = 64
ROPE_THETA = 10000.0
EPS = 1e-6
N_HEADS_A = 8
N_KV_A = 2
GROUP_A = N_HEADS_A // N_KV_A
HEAD_DIM_A = 64
N_HEADS_B = 8
Q_LORA = 768
KV_LORA = 256
NOPE_DIM = 64
ROPE_DIM_B = 32
V_DIM_B = 64
N_EXPERTS = 32
TOP_K = 4
D_FF = D_MODEL
SWIGLU_ALPHA = 1.702
SWIGLU_LIMIT = 7.0
EXPERT_BLOCK = 256

LANE = 128
BF16_SUBLANES = 16
NEG_BIG = -1e30
LOG2E = 1.4426950408889634
ONES_ROWS = BF16_SUBLANES
VMEM_LIMIT = 52 * 1024 * 1024
ATTN_FLAGS = None
ATTN_LAG = 2
ATTN_RING = 4
ATTN_UNROLL = 8

_C_QA = 0
_C_KA = _C_QA + N_HEADS_A * LANE
_C_QLAT = _C_KA + N_KV_A * LANE
_C_KVLAT = _C_QLAT + Q_LORA
_C_KPE = _C_KVLAT + KV_LORA
_C_GATE = _C_KPE + LANE
_C_END = _C_GATE + 2 * D_MODEL

_SH_A, _SC_A, _GT_A, _SH_M, _SC_M, _GT_M = range(6)
MOD_ROWS = 8

_NT = (((1,), (1,)), ((), ()))
_TN = (((0,), (0,)), ((), ()))


def _cparams(*sem, flags=None):
    return pltpu.CompilerParams(dimension_semantics=sem, vmem_limit_bytes=VMEM_LIMIT, flags=flags)


def _rms(x, g):
    return x * lax.rsqrt(jnp.mean(x * x, axis=-1, keepdims=True) + EPS) * g


def _mod_kernel(c_ref, w_ref, b_ref, o_ref):
    c = c_ref[...]
    s = c * jax.nn.sigmoid(c)
    o_ref[...] = jnp.dot(s.astype(BF16), w_ref[...].astype(BF16), preferred_element_type=F32) + b_ref[...]


def _modulation(c, w_mod, b_mod):
    nb, d = c.shape
    rows = -(-nb // 8) * 8
    cp = jnp.zeros((rows, d), F32).at[:nb].set(c)
    n_out = w_mod.shape[1]
    tn = 512
    out = pl.pallas_call(
        _mod_kernel,
        grid=(n_out // tn,),
        in_specs=[pl.BlockSpec((rows, d), lambda j: (0, 0)),
                  pl.BlockSpec((d, tn), lambda j: (0, j)),
                  pl.BlockSpec((1, tn), lambda j: (0, j))],
        out_specs=pl.BlockSpec((rows, tn), lambda j: (0, j)),
        out_shape=jax.ShapeDtypeStruct((rows, n_out), F32),
        compiler_params=_cparams("arbitrary"),
        name="mod",
    )(cp, w_mod, b_mod.reshape(1, n_out))
    mod = out[:nb].reshape(nb, 6, d)
    return jnp.concatenate([mod, jnp.zeros((nb, MOD_ROWS - 6, d), F32)], axis=1)


def _rope(x, cos, sinm, sinp, shift):
    return x * cos + pltpu.roll(x, LANE - shift, 1) * sinm + pltpu.roll(x, shift, 1) * sinp


def _pre_kernel(x_ref, mod_ref, gattn_ref, w1_ref, wvat_ref, gqa_ref, gka_ref, gqlat_ref, wuq_ref,
                gkvlat_ref, wuk_ref, wuvt_ref, ta_ref, tb_ref,
                qa_ref, ka_ref, vta_ref, qb_ref, kb_ref, vtb_ref, sig_ref):
    x = x_ref[...]
    u = _rms(x, gattn_ref[...]) * (1.0 + mod_ref[_SC_A:_SC_A + 1, :]) + mod_ref[_SH_A:_SH_A + 1, :]
    ub = u.astype(BF16)

    def proj(lo, hi):
        return jnp.dot(ub, w1_ref[:, lo:hi], preferred_element_type=F32)

    cos_a, sinm_a, sinp_a = ta_ref[0], ta_ref[1], ta_ref[2]
    cos_b, sinm_b, sinp_b = tb_ref[0], tb_ref[1], tb_ref[2]
    shift_a = HEAD_DIM_A // 4
    shift_b = ROPE_DIM_B // 4

    def head_a(xs, g, scale):
        ms = jnp.sum(xs * xs, axis=-1, keepdims=True) * (1.0 / HEAD_DIM_A)
        xn = xs * lax.rsqrt(ms + EPS) * g
        return (_rope(xn, cos_a, sinm_a, sinp_a, shift_a) * scale).astype(BF16)

    qa = proj(_C_QA, _C_KA)
    for h in range(N_HEADS_A):
        qa_ref[:, h * LANE:(h + 1) * LANE] = head_a(qa[:, h * LANE:(h + 1) * LANE], gqa_ref[...],
                                                    HEAD_DIM_A ** -0.5 * LOG2E)
    ka = proj(_C_KA, _C_QLAT)
    for h in range(N_KV_A):
        ka_ref[:, h * LANE:(h + 1) * LANE] = head_a(ka[:, h * LANE:(h + 1) * LANE], gka_ref[...], 1.0)
    vta = lax.dot_general(wvat_ref[...], ub, _NT, preferred_element_type=F32)
    vta_ref[0, :, :HEAD_DIM_A, :] = vta.astype(BF16).reshape(N_KV_A, HEAD_DIM_A, vta.shape[-1])
    vta_ref[0, :, HEAD_DIM_A:, :] = jnp.ones((N_KV_A, ONES_ROWS, vta.shape[-1]), BF16)

    qn = _rms(proj(_C_QLAT, _C_KVLAT), gqlat_ref[...]).astype(BF16)
    qb = jnp.dot(qn, wuq_ref[...], preferred_element_type=F32)
    scale_b = (NOPE_DIM + ROPE_DIM_B) ** -0.5 * LOG2E
    for h in range(N_HEADS_B):
        sl = slice(h * LANE, (h + 1) * LANE)
        qb_ref[:, sl] = (_rope(qb[:, sl], cos_b, sinm_b, sinp_b, shift_b) * scale_b).astype(BF16)
    kvn = _rms(proj(_C_KVLAT, _C_KPE), gkvlat_ref[...]).astype(BF16)
    kpe = _rope(proj(_C_KPE, _C_GATE), cos_b, sinm_b, sinp_b, shift_b)
    kb = jnp.dot(kvn, wuk_ref[...], preferred_element_type=F32)
    for h in range(N_HEADS_B):
        sl = slice(h * LANE, (h + 1) * LANE)
        kb_ref[:, sl] = (kb[:, sl] + kpe).astype(BF16)
    vtb = lax.dot_general(wuvt_ref[...], kvn, _NT, preferred_element_type=F32)
    vtb_ref[0, :, :V_DIM_B, :] = vtb.astype(BF16).reshape(N_HEADS_B, V_DIM_B, vtb.shape[-1])
    vtb_ref[0, :, V_DIM_B:, :] = jnp.ones((N_HEADS_B, ONES_ROWS, vtb.shape[-1]), BF16)

    sig_ref[...] = jax.nn.sigmoid(proj(_C_GATE, _C_END)).astype(BF16)


def _pre(x2d, mod, wp, tables_a, tables_b, seq, tm):
    n_tok, d = x2d.shape
    n_tiles = n_tok // tm
    tiles_per_seq = seq // tm
    full = lambda a: pl.BlockSpec(a.shape, lambda i: (0,) * a.ndim)
    tok = lambda cols: pl.BlockSpec((tm, cols), lambda i: (i, 0))
    tab = pl.BlockSpec((3, tm, LANE), lambda i: (0, i % tiles_per_seq, 0))
    in_specs = [tok(d),
                pl.BlockSpec((None, MOD_ROWS, d), lambda i: (i // tiles_per_seq, 0, 0)),
                full(wp["g_attn"]), full(wp["w1"]), full(wp["wvat"]), full(wp["g_qa"]), full(wp["g_ka"]),
                full(wp["g_qlat"]), full(wp["wuq"]), full(wp["g_kvlat"]), full(wp["wuk"]), full(wp["wuvt"]),
                tab, tab]
    out_shape = [
        jax.ShapeDtypeStruct((n_tok, N_HEADS_A * LANE), BF16),
        jax.ShapeDtypeStruct((n_tok, N_KV_A * LANE), BF16),
        jax.ShapeDtypeStruct((n_tiles, N_KV_A, HEAD_DIM_A + ONES_ROWS, tm), BF16),
        jax.ShapeDtypeStruct((n_tok, N_HEADS_B * LANE), BF16),
        jax.ShapeDtypeStruct((n_tok, N_HEADS_B * LANE), BF16),
        jax.ShapeDtypeStruct((n_tiles, N_HEADS_B, V_DIM_B + ONES_ROWS, tm), BF16),
        jax.ShapeDtypeStruct((n_tok, 2 * D_MODEL), BF16),
    ]
    out_specs = [tok(N_HEADS_A * LANE), tok(N_KV_A * LANE),
                 pl.BlockSpec((1, N_KV_A, HEAD_DIM_A + ONES_ROWS, tm), lambda i: (i, 0, 0, 0)),
                 tok(N_HEADS_B * LANE), tok(N_HEADS_B * LANE),
                 pl.BlockSpec((1, N_HEADS_B, V_DIM_B + ONES_ROWS, tm), lambda i: (i, 0, 0, 0)),
                 tok(2 * D_MODEL)]
    return pl.pallas_call(
        _pre_kernel, grid=(n_tiles,), in_specs=in_specs, out_specs=out_specs, out_shape=out_shape,
        compiler_params=_cparams("arbitrary"), name="pre",
    )(x2d, mod, wp["g_attn"], wp["w1"], wp["wvat"], wp["g_qa"], wp["g_ka"], wp["g_qlat"], wp["wuq"],
      wp["g_kvlat"], wp["wuk"], wp["wuvt"], tables_a, tables_b)


def _attn_kernel(q_ref, k_ref, vt_ref, o_ref, *bufs, n_sub, tq, tk, n_chunks, dv, unroll):
    s_bufs = bufs[:ATTN_RING]
    p_bufs = bufs[ATTN_RING:]
    lag = ATTN_LAG
    if n_sub == 1:
        q = q_ref[...]
    else:
        q = jnp.concatenate([q_ref[:, j * LANE:(j + 1) * LANE] for j in range(n_sub)], axis=0)
    cols = n_sub * tq

    def scores(i):
        k = k_ref[pl.ds(pl.multiple_of(i * tk, tk), tk), :]
        return lax.dot_general(k, q, _NT, preferred_element_type=F32)

    def pv(c, p_ref):
        return jnp.dot(vt_ref[c], p_ref[...], preferred_element_type=F32)

    def trip(i, u, m, acc, alphas):
        acc = alphas[-1] * acc + pv(jnp.maximum(i - lag, 0), p_bufs[(u - lag) % ATTN_RING])
        s_bufs[(u + lag) % ATTN_RING][...] = scores(jnp.minimum(i + lag, n_chunks - 1))
        s = s_bufs[u % ATTN_RING][...]
        m_new = jnp.maximum(m, jnp.max(s, axis=0, keepdims=True))
        alpha = jnp.exp2(m - m_new)
        p_bufs[u % ATTN_RING][...] = jnp.exp2(s - m_new).astype(BF16)
        return m_new, acc, (alpha,) + alphas[:-1]

    def body(j, carry):
        for u in range(unroll):
            carry = trip(j * unroll + u, u, *carry)
        return carry

    for c in range(lag):
        s_bufs[c][...] = scores(c)
        p_bufs[(-1 - c) % ATTN_RING][...] = jnp.zeros((tk, cols), BF16)
    init = (jnp.full((1, cols), NEG_BIG, F32), jnp.zeros((dv + ONES_ROWS, cols), F32),
            (jnp.ones((1, cols), F32),) * lag)
    _, acc, alphas = lax.fori_loop(0, n_chunks // unroll, body, init)
    for c in range(n_chunks - lag, n_chunks):
        acc = alphas[n_chunks - 1 - c] * acc + pv(c, p_bufs[c % ATTN_RING])
    o = acc[:dv] / acc[dv:dv + 1]
    for j in range(n_sub):
        o_ref[j * dv:(j + 1) * dv, :] = o[:, j * tq:(j + 1) * tq].astype(BF16)


def _attention(q, k, vt, *, n_batch, seq, n_groups, n_sub, tq, dv):
    n_tok = q.shape[0]
    tk = vt.shape[-1]
    n_chunks = seq // tk
    q_tiles = seq // tq
    unroll = ATTN_UNROLL if n_chunks % ATTN_UNROLL == 0 else ATTN_RING
    assert n_chunks % unroll == 0 and unroll % ATTN_RING == 0, "buffer slots must be static per trip"
    kern = functools.partial(_attn_kernel, n_sub=n_sub, tq=tq, tk=tk, n_chunks=n_chunks, dv=dv,
                             unroll=unroll)
    cols = n_sub * tq
    return pl.pallas_call(
        kern,
        grid=(n_batch, n_groups, q_tiles),
        in_specs=[pl.BlockSpec((tq, n_sub * LANE), lambda b, g, i: (b * q_tiles + i, g)),
                  pl.BlockSpec((seq, LANE), lambda b, g, i: (b, g)),
                  pl.BlockSpec((n_chunks, None, dv + ONES_ROWS, tk), lambda b, g, i: (b, g, 0, 0))],
        out_specs=pl.BlockSpec((n_sub * dv, tq), lambda b, g, i: (g, b * q_tiles + i)),
        out_shape=jax.ShapeDtypeStruct((n_groups * n_sub * dv, n_tok), BF16),
        scratch_shapes=[pltpu.VMEM((tk, cols), F32)] * ATTN_RING + [pltpu.VMEM((tk, cols), BF16)] * ATTN_RING,
        compiler_params=_cparams("arbitrary", "arbitrary", "arbitrary", flags=ATTN_FLAGS),
        name="attn",
    )(q, k, vt)


def _post_kernel(x_ref, mod_ref, ota_ref, otb_ref, sig_ref, wbra_ref, wbrb_ref, wout_ref, gmoe_ref,
                 wr_ref, br_ref,
                 x1_ref, u2_ref, eid_ref, rank_ref, gate_ref, cnt_ref, carry_ref):
    i = pl.program_id(0)

    @pl.when(i == 0)
    def _():
        carry_ref[...] = jnp.zeros_like(carry_ref)

    d = D_MODEL
    ya = lax.dot_general(ota_ref[...], wbra_ref[...], _TN, preferred_element_type=F32)
    yb = lax.dot_general(otb_ref[...], wbrb_ref[...], _TN, preferred_element_type=F32)
    mixed = sig_ref[:, :d].astype(F32) * ya + sig_ref[:, d:].astype(F32) * yb
    att = jnp.dot(mixed.astype(BF16), wout_ref[...], preferred_element_type=F32)
    x1 = x_ref[...] + mod_ref[_GT_A:_GT_A + 1, :] * att
    x1_ref[...] = x1
    u2 = _rms(x1, gmoe_ref[...]) * (1.0 + mod_ref[_SC_M:_SC_M + 1, :]) + mod_ref[_SH_M:_SH_M + 1, :]
    u2_ref[...] = u2

    logits = jnp.dot(u2.astype(BF16), wr_ref[...], preferred_element_type=F32) + br_ref[...]
    tm = logits.shape[0]
    lane = lax.broadcasted_iota(jnp.int32, (tm, LANE), 1)
    vals = logits
    top_v, top_i = [], []
    for _ in range(TOP_K):
        mx = jnp.max(vals, axis=-1, keepdims=True)
        idx = jnp.min(jnp.where(vals == mx, lane, LANE), axis=-1, keepdims=True)
        top_v.append(mx)
        top_i.append(idx)
        vals = jnp.where(lane == idx, -jnp.inf, vals)
    ex = [jnp.exp(v - top_v[0]) for v in top_v]
    den = ex[0] + ex[1] + ex[2] + ex[3]

    onehot = [lane == idx for idx in top_i]
    cnt = sum(oh.astype(F32) for oh in onehot)
    r_i = lax.broadcasted_iota(jnp.int32, (tm, tm), 0)
    c_i = lax.broadcasted_iota(jnp.int32, (tm, tm), 1)
    lower = (c_i < r_i).astype(BF16)
    before = jnp.dot(lower, cnt.astype(BF16), preferred_element_type=F32) + carry_ref[...]
    eid = jnp.zeros((tm, LANE), jnp.int32)
    rank = jnp.zeros((tm, LANE), jnp.int32)
    gate = jnp.zeros((tm, LANE), F32)
    for k in range(TOP_K):
        rk = jnp.sum(jnp.where(onehot[k], before, 0.0), axis=-1, keepdims=True).astype(jnp.int32)
        eid = jnp.where(lane == k, top_i[k], eid)
        rank = jnp.where(lane == k, rk, rank)
        gate = jnp.where(lane == k, ex[k] / den, gate)
    eid_ref[...] = eid
    rank_ref[...] = rank
    gate_ref[...] = gate
    carry = carry_ref[...] + jnp.sum(cnt, axis=0, keepdims=True)
    carry_ref[...] = carry
    cnt_ref[...] = carry


def _post(x2d, mod, ota, otb, sig, wp, seq, tm):
    n_tok, d = x2d.shape
    n_tiles = n_tok // tm
    tiles_per_seq = seq // tm
    full = lambda a: pl.BlockSpec(a.shape, lambda i: (0,) * a.ndim)
    tok = lambda cols: pl.BlockSpec((tm, cols), lambda i: (i, 0))
    tcol = lambda rows: pl.BlockSpec((rows, tm), lambda i: (0, i))
    in_specs = [tok(d), pl.BlockSpec((None, MOD_ROWS, d), lambda i: (i // tiles_per_seq, 0, 0)),
                tcol(ota.shape[0]), tcol(otb.shape[0]), tok(2 * d),
                full(wp["wbra"]), full(wp["wbrb"]), full(wp["wout"]), full(wp["g_moe"]),
                full(wp["wr"]), full(wp["br"])]
    out_shape = [jax.ShapeDtypeStruct((n_tok, d), F32), jax.ShapeDtypeStruct((n_tok, d), F32),
                 jax.ShapeDtypeStruct((n_tok, LANE), jnp.int32), jax.ShapeDtypeStruct((n_tok, LANE), jnp.int32),
                 jax.ShapeDtypeStruct((n_tok, LANE), F32), jax.ShapeDtypeStruct((1, LANE), F32)]
    out_specs = [tok(d), tok(d), tok(LANE), tok(LANE), tok(LANE), pl.BlockSpec((1, LANE), lambda i: (0, 0))]
    return pl.pallas_call(
        _post_kernel, grid=(n_tiles,), in_specs=in_specs, out_specs=out_specs, out_shape=out_shape,
        scratch_shapes=[pltpu.VMEM((1, LANE), F32)],
        compiler_params=_cparams("arbitrary"), name="post",
    )(x2d, mod, ota, otb, sig, wp["wbra"], wp["wbrb"], wp["wout"], wp["g_moe"], wp["wr"], wp["br"])


def _row_copy(src, dst, sem):
    return pltpu.make_async_copy(src, dst, sem)


def _dispatch_kernel(dest_ref, u_ref, xs_in_ref, xs_ref, sem, *, tm):
    del xs_in_ref

    def issue(r, c):
        for k in range(TOP_K):
            d = dest_ref[0, 0, r * TOP_K + k]
            _row_copy(u_ref.at[pl.ds(r, 1)], xs_ref.at[pl.ds(d, 1)], sem).start()
        return c

    lax.fori_loop(0, tm, issue, 0)

    def drain(r, c):
        for k in range(TOP_K):
            _row_copy(u_ref.at[pl.ds(0, 1)], xs_ref.at[pl.ds(0, 1)], sem).wait()
        return c

    lax.fori_loop(0, tm, drain, 0)


def _dispatch(u2, dest, cap, tm):
    n_tok, d = u2.shape
    n_tiles = n_tok // tm
    dest3 = dest.reshape(n_tiles, 1, tm * TOP_K)
    xs0 = jnp.zeros((cap, d), u2.dtype)
    return pl.pallas_call(
        functools.partial(_dispatch_kernel, tm=tm),
        grid=(n_tiles,),
        in_specs=[pl.BlockSpec((1, 1, tm * TOP_K), lambda i: (i, 0, 0), memory_space=pltpu.SMEM),
                  pl.BlockSpec((tm, d), lambda i: (i, 0)),
                  pl.BlockSpec(memory_space=pl.ANY)],
        out_specs=pl.BlockSpec(memory_space=pl.ANY),
        out_shape=jax.ShapeDtypeStruct((cap, d), u2.dtype),
        scratch_shapes=[pltpu.SemaphoreType.DMA(())],
        input_output_aliases={2: 0},
        compiler_params=_cparams("arbitrary"), name="dispatch",
    )(dest3, u2, xs0)


def _expert_kernel(be_ref, valid_ref, xs_ref, wgu_ref, bgu_ref, wd_ref, bd_ref, y_ref):
    b = pl.program_id(0)

    @pl.when(valid_ref[b] == 1)
    def _():
        x = xs_ref[...].astype(BF16)
        h = jnp.dot(x, wgu_ref[0], preferred_element_type=F32) + bgu_ref[0]
        hg = jnp.minimum(h[:, :D_FF], SWIGLU_LIMIT)
        hl = jnp.clip(h[:, D_FF:], -SWIGLU_LIMIT, SWIGLU_LIMIT)
        a = hg * jax.nn.sigmoid(SWIGLU_ALPHA * hg) * (hl + 1.0)
        y_ref[...] = jnp.dot(a.astype(BF16), wd_ref[0], preferred_element_type=F32) + bd_ref[0]

    @pl.when(valid_ref[b] == 0)
    def _():
        y_ref[...] = jnp.zeros_like(y_ref)


def _experts(xs, block_e, block_valid, wp):
    cap, d = xs.shape
    n_blocks = cap // EXPERT_BLOCK
    grid_spec = pltpu.PrefetchScalarGridSpec(
        num_scalar_prefetch=2,
        grid=(n_blocks,),
        in_specs=[pl.BlockSpec((EXPERT_BLOCK, d), lambda b, be, bv: (b, 0)),
                  pl.BlockSpec((1, d, 2 * D_FF), lambda b, be, bv: (be[b], 0, 0)),
                  pl.BlockSpec((1, 1, 2 * D_FF), lambda b, be, bv: (be[b], 0, 0)),
                  pl.BlockSpec((1, D_FF, d), lambda b, be, bv: (be[b], 0, 0)),
                  pl.BlockSpec((1, 1, d), lambda b, be, bv: (be[b], 0, 0))],
        out_specs=pl.BlockSpec((EXPERT_BLOCK, d), lambda b, be, bv: (b, 0)),
    )
    return pl.pallas_call(
        _expert_kernel, grid_spec=grid_spec,
        out_shape=jax.ShapeDtypeStruct((cap, d), F32),
        compiler_params=_cparams("arbitrary"), name="experts",
    )(block_e, block_valid, xs, wp["wgu"], wp["bgu"], wp["wd"], wp["bd"])


def _combine_kernel(dest_ref, x1_ref, gate_ref, mod_ref, gfin_ref, y_hbm, o_ref, buf, sem, *, tm):
    def issue(r, c):
        for k in range(TOP_K):
            d = dest_ref[0, 0, r * TOP_K + k]
            _row_copy(y_hbm.at[pl.ds(d, 1)], buf.at[k, pl.ds(r, 1)], sem).start()
        return c

    lax.fori_loop(0, tm, issue, 0)

    def drain(r, c):
        for k in range(TOP_K):
            _row_copy(y_hbm.at[pl.ds(0, 1)], buf.at[k, pl.ds(0, 1)], sem).wait()
        return c

    lax.fori_loop(0, tm, drain, 0)
    gate = gate_ref[...]
    y = gate[:, 0:1] * buf[0]
    for k in range(1, TOP_K):
        y = y + gate[:, k:k + 1] * buf[k]
    x2 = x1_ref[...] + mod_ref[_GT_M:_GT_M + 1, :] * y
    o_ref[...] = _rms(x2, gfin_ref[...])


def _combine(x1, gate, mod, g_final, yb, dest, seq, tm):
    n_tok, d = x1.shape
    n_tiles = n_tok // tm
    tiles_per_seq = seq // tm
    dest3 = dest.reshape(n_tiles, 1, tm * TOP_K)
    return pl.pallas_call(
        functools.partial(_combine_kernel, tm=tm),
        grid=(n_tiles,),
        in_specs=[pl.BlockSpec((1, 1, tm * TOP_K), lambda i: (i, 0, 0), memory_space=pltpu.SMEM),
                  pl.BlockSpec((tm, d), lambda i: (i, 0)),
                  pl.BlockSpec((tm, LANE), lambda i: (i, 0)),
                  pl.BlockSpec((None, MOD_ROWS, d), lambda i: (i // tiles_per_seq, 0, 0)),
                  pl.BlockSpec((1, d), lambda i: (0, 0)),
                  pl.BlockSpec(memory_space=pl.ANY)],
        out_specs=pl.BlockSpec((tm, d), lambda i: (i, 0)),
        out_shape=jax.ShapeDtypeStruct((n_tok, d), F32),
        scratch_shapes=[pltpu.VMEM((TOP_K, tm, d), F32), pltpu.SemaphoreType.DMA(())],
        compiler_params=_cparams("arbitrary"), name="combine",
    )(dest3, x1, gate, mod, g_final, yb)


def _slot_cols(w, n_heads, width, offset=0):
    k = w.shape[0]
    w3 = w.reshape(k, n_heads, width)
    out = jnp.zeros((k, n_heads, LANE), w.dtype).at[:, :, offset:offset + width].set(w3)
    return out.reshape(k, n_heads * LANE)


def _prep_weights(g_attn, g_moe, w_in, g_qn, g_kn, g_qlat, w_uq, g_kvlat, w_ukv, w_br_a, w_br_b, w_out,
                  w_router, b_router, w_gu, b_gu, w_down, b_down):
    d = D_MODEL
    wa = N_HEADS_A * HEAD_DIM_A
    wk = N_KV_A * HEAD_DIM_A
    c0 = 0
    w_qa = w_in[:, c0:c0 + wa]; c0 += wa
    w_ka = w_in[:, c0:c0 + wk]; c0 += wk
    w_va = w_in[:, c0:c0 + wk]; c0 += wk
    w_ql = w_in[:, c0:c0 + Q_LORA]; c0 += Q_LORA
    w_kvl = w_in[:, c0:c0 + KV_LORA]; c0 += KV_LORA
    w_kpe = w_in[:, c0:c0 + ROPE_DIM_B]; c0 += ROPE_DIM_B
    w_gate = w_in[:, c0:]
    w1 = jnp.concatenate([
        _slot_cols(w_qa, N_HEADS_A, HEAD_DIM_A), _slot_cols(w_ka, N_KV_A, HEAD_DIM_A), w_ql, w_kvl,
        _slot_cols(w_kpe, 1, ROPE_DIM_B, NOPE_DIM), w_gate], axis=1).astype(BF16)
    ukv = w_ukv.reshape(KV_LORA, N_HEADS_B, NOPE_DIM + V_DIM_B)
    w_uk = ukv[:, :, :NOPE_DIM].reshape(KV_LORA, N_HEADS_B * NOPE_DIM)
    w_uv = ukv[:, :, NOPE_DIM:].reshape(KV_LORA, N_HEADS_B * V_DIM_B)
    pad_gain = lambda g: jnp.zeros((1, LANE), F32).at[0, :g.shape[0]].set(g)
    return {
        "g_attn": g_attn.reshape(1, d), "g_moe": g_moe.reshape(1, d),
        "w1": w1, "wvat": w_va.T.astype(BF16),
        "g_qa": pad_gain(g_qn), "g_ka": pad_gain(g_kn),
        "g_qlat": g_qlat.reshape(1, Q_LORA), "g_kvlat": g_kvlat.reshape(1, KV_LORA),
        "wuq": _slot_cols(w_uq, N_HEADS_B, NOPE_DIM + ROPE_DIM_B).astype(BF16),
        "wuk": _slot_cols(w_uk, N_HEADS_B, NOPE_DIM).astype(BF16),
        "wuvt": w_uv.T.astype(BF16),
        "wbra": w_br_a.astype(BF16), "wbrb": w_br_b.astype(BF16), "wout": w_out.astype(BF16),
        "wr": jnp.zeros((d, LANE), F32).at[:, :N_EXPERTS].set(w_router).astype(BF16),
        "br": jnp.full((1, LANE), NEG_BIG, F32).at[0, :N_EXPERTS].set(b_router),
        "wgu": w_gu.astype(BF16), "bgu": b_gu.reshape(N_EXPERTS, 1, 2 * D_FF),
        "wd": w_down.astype(BF16), "bd": b_down.reshape(N_EXPERTS, 1, d),
    }


def _rope_tables(seq, rot_dim, offset):
    half = rot_dim // 2
    quarter = half // 2
    t = jnp.arange(seq, dtype=jnp.int32)
    row = (t // GRID_W).astype(F32)
    col = (t % GRID_W).astype(F32)
    inv = ROPE_THETA ** (-jnp.arange(0, half, 2, dtype=F32) / half)
    ar, ac = row[:, None] * inv, col[:, None] * inv
    z = jnp.zeros((seq, quarter), F32)
    cos = jnp.concatenate([jnp.cos(ar), jnp.cos(ar), jnp.cos(ac), jnp.cos(ac)], axis=1)
    sinm = jnp.concatenate([-jnp.sin(ar), z, -jnp.sin(ac), z], axis=1)
    sinp = jnp.concatenate([z, jnp.sin(ar), z, jnp.sin(ac)], axis=1)

    def place(a, fill):
        return jnp.full((seq, LANE), fill, F32).at[:, offset:offset + rot_dim].set(a)

    return jnp.stack([place(cos, 1.0), place(sinm, 0.0), place(sinp, 0.0)])


def _trunk(x, mod, wp, g_final):
    n_batch, seq, d = x.shape
    n_tok = n_batch * seq
    x2d = x.reshape(n_tok, d)
    tm = min(256, seq)
    ta = _rope_tables(seq, HEAD_DIM_A, 0)
    tb = _rope_tables(seq, ROPE_DIM_B, NOPE_DIM)

    qa, ka, vta, qb, kb, vtb, sig = _pre(x2d, mod, wp, ta, tb, seq, tm)
    ota = _attention(qa, ka, vta, n_batch=n_batch, seq=seq, n_groups=N_KV_A, n_sub=GROUP_A,
                     tq=min(128, seq), dv=HEAD_DIM_A)
    otb = _attention(qb, kb, vtb, n_batch=n_batch, seq=seq, n_groups=N_HEADS_B, n_sub=1,
                     tq=min(512, seq), dv=V_DIM_B)
    x1, u2, eid, rank, gate, cnt = _post(x2d, mod, ota, otb, sig, wp, seq, tm)

    n_assign = n_tok * TOP_K
    n_blocks = -(-n_assign // EXPERT_BLOCK) + N_EXPERTS
    cap = n_blocks * EXPERT_BLOCK
    counts = cnt[0, :N_EXPERTS].astype(jnp.int32)
    padded = (counts + EXPERT_BLOCK - 1) // EXPERT_BLOCK * EXPERT_BLOCK
    pad_end = jnp.cumsum(padded)
    pad_start = pad_end - padded
    dest = pad_start[eid[:, :TOP_K]] + rank[:, :TOP_K]
    block_lo = jnp.arange(n_blocks, dtype=jnp.int32) * EXPERT_BLOCK
    block_e = jnp.minimum(jnp.sum(block_lo[:, None] >= pad_end[None, :], axis=1), N_EXPERTS - 1).astype(jnp.int32)
    block_valid = (block_lo < pad_end[-1]).astype(jnp.int32)

    xs = _dispatch(u2, dest, cap, tm)
    yb = _experts(xs, block_e, block_valid, wp)
    out = _combine(x1, gate, mod, g_final.reshape(1, d), yb, dest, seq, min(128, seq))
    return out.reshape(n_batch, seq, d)


def kernel(x_prompt, x_sample, c_prompt, c_sample, w_mod, b_mod, g_attn, g_moe, w_in, g_qn, g_kn, g_qlat,
           w_uq, g_kvlat, w_ukv, w_br_a, w_br_b, w_out, w_router, b_router, w_gu, b_gu, w_down, b_down,
           g_final):
    assert w_mod.shape[0] == 1, "single-layer trunk"
    wp = _prep_weights(g_attn[0], g_moe[0], w_in[0], g_qn[0], g_kn[0], g_qlat[0], w_uq[0], g_kvlat[0],
                       w_ukv[0], w_br_a[0], w_br_b[0], w_out[0], w_router[0], b_router[0], w_gu[0],
                       b_gu[0], w_down[0], b_down[0])
    nb_p = c_prompt.shape[0]
    mod = _modulation(jnp.concatenate([c_prompt, c_sample], axis=0), w_mod[0], b_mod[0])
    y_prompt = _trunk(x_prompt, mod[:nb_p], wp, g_final)
    y_sample = _trunk(x_sample, mod[nb_p:], wp, g_final)
    return (y_prompt, y_sample)
```

```python
import functools

import jax
import jax.numpy as jnp
from jax import lax
from jax.experimental import pallas as pl
from jax.experimental.pallas import tpu as pltpu

F32 = jnp.float32
BF16 = jnp.bfloat16

D_MODEL = 1024
GRID_W = 64
ROPE_THETA = 10000.0
EPS = 1e-6
N_HEADS_A = 8
N_KV_A = 2
GROUP_A = N_HEADS_A // N_KV_A
HEAD_DIM_A = 64
N_HEADS_B = 8
Q_LORA = 768
KV_LORA = 256
NOPE_DIM = 64
ROPE_DIM_B = 32
V_DIM_B = 64
N_EXPERTS = 32
TOP_K = 4
D_FF = D_MODEL
SWIGLU_ALPHA = 1.702
SWIGLU_LIMIT = 7.0
EXPERT_BLOCK = 256

LANE = 128
BF16_SUBLANES = 16
NEG_BIG = -1e30
LOG2E = 1.4426950408889634
ONES_ROWS = BF16_SUBLANES
VMEM_LIMIT = 52 * 1024 * 1024
ATTN_FLAGS = None
ATTN_LAG = 2
ATTN_RING = 4
ATTN_UNROLL = 32
ROW_LOOP_UNROLL = 4
BASE_LANE = LANE - 1
DEN_MIN, DEN_MAX = 1e-30, 1e30

_C_QA = 0
_C_KA = _C_QA + N_HEADS_A * LANE
_C_QLAT = _C_KA + N_KV_A * LANE
_C_KVLAT = _C_QLAT + Q_LORA
_C_KPE = _C_KVLAT + KV_LORA
_C_GATE = _C_KPE + LANE
_C_END = _C_GATE + 2 * D_MODEL

_SH_A, _SC_A, _GT_A, _SH_M, _SC_M, _GT_M = range(6)
MOD_ROWS = 8

_NT = (((1,), (1,)), ((), ()))
_TN = (((0,), (0,)), ((), ()))


def _cparams(*sem, flags=None):
    return pltpu.CompilerParams(dimension_semantics=sem, vmem_limit_bytes=VMEM_LIMIT, flags=flags)


def _rms(x, g):
    return x * lax.rsqrt(jnp.mean(x * x, axis=-1, keepdims=True) + EPS) * g


def _mod_kernel(c_ref, w_ref, b_ref, o_ref):
    c = c_ref[...]
    s = c * jax.nn.sigmoid(c)
    o_ref[...] = jnp.dot(s.astype(BF16), w_ref[...].astype(BF16), preferred_element_type=F32) + b_ref[...]


def _modulation(c, w_mod, b_mod):
    nb, d = c.shape
    rows = -(-nb // 8) * 8
    cp = jnp.zeros((rows, d), F32).at[:nb].set(c)
    n_out = w_mod.shape[1]
    tn = 512
    out = pl.pallas_call(
        _mod_kernel,
        grid=(n_out // tn,),
        in_specs=[pl.BlockSpec((rows, d), lambda j: (0, 0)),
                  pl.BlockSpec((d, tn), lambda j: (0, j)),
                  pl.BlockSpec((1, tn), lambda j: (0, j))],
        out_specs=pl.BlockSpec((rows, tn), lambda j: (0, j)),
        out_shape=jax.ShapeDtypeStruct((rows, n_out), F32),
        compiler_params=_cparams("arbitrary"),
        name="mod",
    )(cp, w_mod, b_mod.reshape(1, n_out))
    mod = out[:nb].reshape(nb, 6, d)
    return jnp.concatenate([mod, jnp.zeros((nb, MOD_ROWS - 6, d), F32)], axis=1)


def _rope(x, cos, sinm, sinp, shift):
    return x * cos + pltpu.roll(x, LANE - shift, 1) * sinm + pltpu.roll(x, shift, 1) * sinp


def _pre_kernel(x_ref, mod_ref, gattn_ref, w1_ref, wvat_ref, gqa_ref, gka_ref, gqlat_ref, wuq_ref,
                gkvlat_ref, wuk_ref, wuvt_ref, ta_ref, tb_ref,
                qa_ref, ka_ref, vta_ref, qb_ref, kb_ref, vtb_ref, sig_ref):
    x = x_ref[...]
    u = _rms(x, gattn_ref[...]) * (1.0 + mod_ref[_SC_A:_SC_A + 1, :]) + mod_ref[_SH_A:_SH_A + 1, :]
    ub = u.astype(BF16)

    def proj(lo, hi):
        return jnp.dot(ub, w1_ref[:, lo:hi], preferred_element_type=F32)

    cos_a, sinm_a, sinp_a = ta_ref[0], ta_ref[1], ta_ref[2]
    cos_b, sinm_b, sinp_b = tb_ref[0], tb_ref[1], tb_ref[2]
    shift_a = HEAD_DIM_A // 4
    shift_b = ROPE_DIM_B // 4

    def head_a(xs, g, scale):
        ms = jnp.sum(xs * xs, axis=-1, keepdims=True) * (1.0 / HEAD_DIM_A)
        xn = xs * lax.rsqrt(ms + EPS) * g
        return (_rope(xn, cos_a, sinm_a, sinp_a, shift_a) * scale).astype(BF16)

    qa = proj(_C_QA, _C_KA)
    for h in range(N_HEADS_A):
        qa_ref[:, h * LANE:(h + 1) * LANE] = head_a(qa[:, h * LANE:(h + 1) * LANE], gqa_ref[...],
                                                    HEAD_DIM_A ** -0.5 * LOG2E)
    ka = proj(_C_KA, _C_QLAT)
    base_lane = lax.broadcasted_iota(jnp.int32, (x.shape[0], LANE), 1) == BASE_LANE
    for h in range(N_KV_A):
        kh = head_a(ka[:, h * LANE:(h + 1) * LANE], gka_ref[...], 1.0)
        ka_ref[:, h * LANE:(h + 1) * LANE] = jnp.where(base_lane, 1.0, kh).astype(BF16)
    vta = lax.dot_general(wvat_ref[...], ub, _NT, preferred_element_type=F32)
    vta_ref[0, :, :HEAD_DIM_A, :] = vta.astype(BF16).reshape(N_KV_A, HEAD_DIM_A, vta.shape[-1])
    vta_ref[0, :, HEAD_DIM_A:, :] = jnp.ones((N_KV_A, ONES_ROWS, vta.shape[-1]), BF16)

    qn = _rms(proj(_C_QLAT, _C_KVLAT), gqlat_ref[...]).astype(BF16)
    qb = jnp.dot(qn, wuq_ref[...], preferred_element_type=F32)
    scale_b = (NOPE_DIM + ROPE_DIM_B) ** -0.5 * LOG2E
    for h in range(N_HEADS_B):
        sl = slice(h * LANE, (h + 1) * LANE)
        qb_ref[:, sl] = (_rope(qb[:, sl], cos_b, sinm_b, sinp_b, shift_b) * scale_b).astype(BF16)
    kvn = _rms(proj(_C_KVLAT, _C_KPE), gkvlat_ref[...]).astype(BF16)
    kpe = _rope(proj(_C_KPE, _C_GATE), cos_b, sinm_b, sinp_b, shift_b)
    kb = jnp.dot(kvn, wuk_ref[...], preferred_element_type=F32)
    for h in range(N_HEADS_B):
        sl = slice(h * LANE, (h + 1) * LANE)
        kb_ref[:, sl] = jnp.where(base_lane, 1.0, kb[:, sl] + kpe).astype(BF16)
    vtb = lax.dot_general(wuvt_ref[...], kvn, _NT, preferred_element_type=F32)
    vtb_ref[0, :, :V_DIM_B, :] = vtb.astype(BF16).reshape(N_HEADS_B, V_DIM_B, vtb.shape[-1])
    vtb_ref[0, :, V_DIM_B:, :] = jnp.ones((N_HEADS_B, ONES_ROWS, vtb.shape[-1]), BF16)

    sig_ref[...] = jax.nn.sigmoid(proj(_C_GATE, _C_END)).astype(BF16)


def _pre(x2d, mod, wp, tables_a, tables_b, seq, tm):
    n_tok, d = x2d.shape
    n_tiles = n_tok // tm
    tiles_per_seq = seq // tm
    full = lambda a: pl.BlockSpec(a.shape, lambda i: (0,) * a.ndim)
    tok = lambda cols: pl.BlockSpec((tm, cols), lambda i: (i, 0))
    tab = pl.BlockSpec((3, tm, LANE), lambda i: (0, i % tiles_per_seq, 0))
    in_specs = [tok(d),
                pl.BlockSpec((None, MOD_ROWS, d), lambda i: (i // tiles_per_seq, 0, 0)),
                full(wp["g_attn"]), full(wp["w1"]), full(wp["wvat"]), full(wp["g_qa"]), full(wp["g_ka"]),
                full(wp["g_qlat"]), full(wp["wuq"]), full(wp["g_kvlat"]), full(wp["wuk"]), full(wp["wuvt"]),
                tab, tab]
    out_shape = [
        jax.ShapeDtypeStruct((n_tok, N_HEADS_A * LANE), BF16),
        jax.ShapeDtypeStruct((n_tok, N_KV_A * LANE), BF16),
        jax.ShapeDtypeStruct((n_tiles, N_KV_A, HEAD_DIM_A + ONES_ROWS, tm), BF16),
        jax.ShapeDtypeStruct((n_tok, N_HEADS_B * LANE), BF16),
        jax.ShapeDtypeStruct((n_tok, N_HEADS_B * LANE), BF16),
        jax.ShapeDtypeStruct((n_tiles, N_HEADS_B, V_DIM_B + ONES_ROWS, tm), BF16),
        jax.ShapeDtypeStruct((n_tok, 2 * D_MODEL), BF16),
    ]
    out_specs = [tok(N_HEADS_A * LANE), tok(N_KV_A * LANE),
                 pl.BlockSpec((1, N_KV_A, HEAD_DIM_A + ONES_ROWS, tm), lambda i: (i, 0, 0, 0)),
                 tok(N_HEADS_B * LANE), tok(N_HEADS_B * LANE),
                 pl.BlockSpec((1, N_HEADS_B, V_DIM_B + ONES_ROWS, tm), lambda i: (i, 0, 0, 0)),
                 tok(2 * D_MODEL)]
    return pl.pallas_call(
        _pre_kernel, grid=(n_tiles,), in_specs=in_specs, out_specs=out_specs, out_shape=out_shape,
        compiler_params=_cparams("arbitrary"), name="pre",
    )(x2d, mod, wp["g_attn"], wp["w1"], wp["wvat"], wp["g_qa"], wp["g_ka"], wp["g_qlat"], wp["wuq"],
      wp["g_kvlat"], wp["wuk"], wp["wuvt"], tables_a, tables_b)


def _attn_kernel(q_ref, k_ref, vt_ref, o_ref, kmax_ref, *p_bufs, n_sub, tq, tk, n_chunks, dv, unroll):
    lag = ATTN_LAG
    cols = n_sub * tq
    if n_sub == 1:
        q = q_ref[...]
    else:
        q = jnp.concatenate([q_ref[:, j * LANE:(j + 1) * LANE] for j in range(n_sub)], axis=0)

    def key_chunk(i):
        return k_ref[pl.ds(pl.multiple_of(i * tk, tk), tk), :]

    @pl.when(pl.program_id(2) == 0)
    def _():
        def norm_body(c, mx):
            kc = key_chunk(c).astype(F32)
            n2 = jnp.sum(kc * kc, axis=-1, keepdims=True)
            return jnp.maximum(mx, jnp.max(n2, axis=0, keepdims=True))

        kmax2 = lax.fori_loop(0, n_chunks, norm_body, jnp.zeros((1, 1), F32))
        kmax_ref[...] = jnp.broadcast_to(jnp.sqrt(jnp.maximum(kmax2 - 1.0, 0.0)), kmax_ref.shape)

    qt = q.astype(F32).T
    base = jnp.sqrt(jnp.sum(qt * qt, axis=0, keepdims=True)) * kmax_ref[0:1, 0:1]
    row = lax.broadcasted_iota(jnp.int32, qt.shape, 0)
    qt_aug = jnp.where(row == BASE_LANE, -base, qt).astype(BF16)

    def pv(c, p_ref):
        return jnp.dot(vt_ref[c], p_ref[...], preferred_element_type=F32)

    def trip(i, u, acc):
        acc = acc + pv(jnp.maximum(i - lag, 0), p_bufs[(u - lag) % ATTN_RING])
        s = jnp.dot(key_chunk(i), qt_aug, preferred_element_type=F32)
        p_bufs[u % ATTN_RING][...] = jnp.exp2(s).astype(BF16)
        return acc

    def body(j, acc):
        for u in range(unroll):
            acc = trip(j * unroll + u, u, acc)
        return acc

    for c in range(lag):
        p_bufs[(-1 - c) % ATTN_RING][...] = jnp.zeros((tk, cols), BF16)
    acc = lax.fori_loop(0, n_chunks // unroll, body, jnp.zeros((dv + ONES_ROWS, cols), F32))
    for c in range(n_chunks - lag, n_chunks):
        acc = acc + pv(c, p_bufs[c % ATTN_RING])

    def emit(acc):
        o = acc[:dv] / acc[dv:dv + 1]
        for j in range(n_sub):
            o_ref[j * dv:(j + 1) * dv, :] = o[:, j * tq:(j + 1) * tq].astype(BF16)

    emit(acc)
    den = acc[dv:dv + 1]
    safe = jnp.logical_and(den > DEN_MIN, den < DEN_MAX)
    n_unsafe = jnp.sum(jnp.where(safe, 0.0, 1.0))

    @pl.when(n_unsafe > 0.0)
    def _():
        def exact_body(i, carry):
            m, acc = carry
            s = lax.dot_general(key_chunk(i), q, _NT, preferred_element_type=F32)
            m_new = jnp.maximum(m, jnp.max(s, axis=0, keepdims=True))
            p = jnp.exp2(s - m_new).astype(BF16)
            return m_new, jnp.exp2(m - m_new) * acc + jnp.dot(vt_ref[i], p, preferred_element_type=F32)

        init = (jnp.full((1, cols), NEG_BIG, F32), jnp.zeros((dv + ONES_ROWS, cols), F32))
        emit(lax.fori_loop(0, n_chunks, exact_body, init)[1])


def _attention(q, k, vt, *, n_batch, seq, n_groups, n_sub, tq, dv):
    n_tok = q.shape[0]
    tk = vt.shape[-1]
    n_chunks = seq // tk
    q_tiles = seq // tq
    unroll = ATTN_UNROLL
    while n_chunks % unroll:
        unroll //= 2
    assert unroll % ATTN_RING == 0, "buffer slots must be static per trip"
    kern = functools.partial(_attn_kernel, n_sub=n_sub, tq=tq, tk=tk, n_chunks=n_chunks, dv=dv,
                             unroll=unroll)
    cols = n_sub * tq
    return pl.pallas_call(
        kern,
        grid=(n_batch, n_groups, q_tiles),
        in_specs=[pl.BlockSpec((tq, n_sub * LANE), lambda b, g, i: (b * q_tiles + i, g)),
                  pl.BlockSpec((seq, LANE), lambda b, g, i: (b, g)),
                  pl.BlockSpec((n_chunks, None, dv + ONES_ROWS, tk), lambda b, g, i: (b, g, 0, 0))],
        out_specs=pl.BlockSpec((n_sub * dv, tq), lambda b, g, i: (g, b * q_tiles + i)),
        out_shape=jax.ShapeDtypeStruct((n_groups * n_sub * dv, n_tok), BF16),
        scratch_shapes=[pltpu.VMEM((8, LANE), F32)] + [pltpu.VMEM((tk, cols), BF16)] * ATTN_RING,
        compiler_params=_cparams("arbitrary", "arbitrary", "arbitrary", flags=ATTN_FLAGS),
        name="attn",
    )(q, k, vt)


def _post_kernel(x_ref, mod_ref, ota_ref, otb_ref, sig_ref, wbra_ref, wbrb_ref, wout_ref, gmoe_ref,
                 wr_ref, br_ref,
                 x1_ref, u2_ref, eid_ref, rank_ref, gate_ref, cnt_ref, carry_ref):
    i = pl.program_id(0)

    @pl.when(i == 0)
    def _():
        carry_ref[...] = jnp.zeros_like(carry_ref)

    d = D_MODEL
    ya = lax.dot_general(ota_ref[...], wbra_ref[...], _TN, preferred_element_type=F32)
    yb = lax.dot_general(otb_ref[...], wbrb_ref[...], _TN, preferred_element_type=F32)
    mixed = sig_ref[:, :d].astype(F32) * ya + sig_ref[:, d:].astype(F32) * yb
    att = jnp.dot(mixed.astype(BF16), wout_ref[...], preferred_element_type=F32)
    x1 = x_ref[...] + mod_ref[_GT_A:_GT_A + 1, :] * att
    x1_ref[...] = x1
    u2 = _rms(x1, gmoe_ref[...]) * (1.0 + mod_ref[_SC_M:_SC_M + 1, :]) + mod_ref[_SH_M:_SH_M + 1, :]
    u2_ref[...] = u2

    logits = jnp.dot(u2.astype(BF16), wr_ref[...], preferred_element_type=F32) + br_ref[...]
    tm = logits.shape[0]
    lane = lax.broadcasted_iota(jnp.int32, (tm, LANE), 1)
    vals = logits
    top_v, top_i = [], []
    for _ in range(TOP_K):
        mx = jnp.max(vals, axis=-1, keepdims=True)
        idx = jnp.min(jnp.where(vals == mx, lane, LANE), axis=-1, keepdims=True)
        top_v.append(mx)
        top_i.append(idx)
        vals = jnp.where(lane == idx, -jnp.inf, vals)
    ex = [jnp.exp(v - top_v[0]) for v in top_v]
    den = ex[0] + ex[1] + ex[2] + ex[3]

    onehot = [lane == idx for idx in top_i]
    cnt = sum(oh.astype(F32) for oh in onehot)
    r_i = lax.broadcasted_iota(jnp.int32, (tm, tm), 0)
    c_i = lax.broadcasted_iota(jnp.int32, (tm, tm), 1)
    lower = (c_i < r_i).astype(BF16)
    before = jnp.dot(lower, cnt.astype(BF16), preferred_element_type=F32) + carry_ref[...]
    eid = jnp.zeros((tm, LANE), jnp.int32)
    rank = jnp.zeros((tm, LANE), jnp.int32)
    gate = jnp.zeros((tm, LANE), F32)
    for k in range(TOP_K):
        rk = jnp.sum(jnp.where(onehot[k], before, 0.0), axis=-1, keepdims=True).astype(jnp.int32)
        eid = jnp.where(lane == k, top_i[k], eid)
        rank = jnp.where(lane == k, rk, rank)
        gate = jnp.where(lane == k, ex[k] / den, gate)
    eid_ref[...] = eid
    rank_ref[...] = rank
    gate_ref[...] = gate
    carry = carry_ref[...] + jnp.sum(cnt, axis=0, keepdims=True)
    carry_ref[...] = carry
    cnt_ref[...] = carry


def _post(x2d, mod, ota, otb, sig, wp, seq, tm):
    n_tok, d = x2d.shape
    n_tiles = n_tok // tm
    tiles_per_seq = seq // tm
    full = lambda a: pl.BlockSpec(a.shape, lambda i: (0,) * a.ndim)
    tok = lambda cols: pl.BlockSpec((tm, cols), lambda i: (i, 0))
    tcol = lambda rows: pl.BlockSpec((rows, tm), lambda i: (0, i))
    in_specs = [tok(d), pl.BlockSpec((None, MOD_ROWS, d), lambda i: (i // tiles_per_seq, 0, 0)),
                tcol(ota.shape[0]), tcol(otb.shape[0]), tok(2 * d),
                full(wp["wbra"]), full(wp["wbrb"]), full(wp["wout"]), full(wp["g_moe"]),
                full(wp["wr"]), full(wp["br"])]
    out_shape = [jax.ShapeDtypeStruct((n_tok, d), F32), jax.ShapeDtypeStruct((n_tok, d), F32),
                 jax.ShapeDtypeStruct((n_tok, LANE), jnp.int32), jax.ShapeDtypeStruct((n_tok, LANE), jnp.int32),
                 jax.ShapeDtypeStruct((n_tok, LANE), F32), jax.ShapeDtypeStruct((1, LANE), F32)]
    out_specs = [tok(d), tok(d), tok(LANE), tok(LANE), tok(LANE), pl.BlockSpec((1, LANE), lambda i: (0, 0))]
    return pl.pallas_call(
        _post_kernel, grid=(n_tiles,), in_specs=in_specs, out_specs=out_specs, out_shape=out_shape,
        scratch_shapes=[pltpu.VMEM((1, LANE), F32)],
        compiler_params=_cparams("arbitrary"), name="post",
    )(x2d, mod, ota, otb, sig, wp["wbra"], wp["wbrb"], wp["wout"], wp["g_moe"], wp["wr"], wp["br"])


def _row_copy(src, dst, sem):
    return pltpu.make_async_copy(src, dst, sem)


def _dispatch_kernel(dest_ref, u_ref, xs_in_ref, xs_ref, sem, *, tm):
    del xs_in_ref

    def issue(r, c):
        for k in range(TOP_K):
            d = dest_ref[0, 0, r * TOP_K + k]
            _row_copy(u_ref.at[pl.ds(r, 1)], xs_ref.at[pl.ds(d, 1)], sem).start(priority=k % 2)
        return c

    lax.fori_loop(0, tm, issue, 0, unroll=ROW_LOOP_UNROLL)

    def drain(r, c):
        for k in range(TOP_K):
            _row_copy(u_ref.at[pl.ds(0, 1)], xs_ref.at[pl.ds(0, 1)], sem).wait()
        return c

    lax.fori_loop(0, tm, drain, 0, unroll=ROW_LOOP_UNROLL)


def _dispatch(u2, dest, cap, tm):
    n_tok, d = u2.shape
    n_tiles = n_tok // tm
    dest3 = dest.reshape(n_tiles, 1, tm * TOP_K)
    xs0 = jnp.zeros((cap, d), u2.dtype)
    return pl.pallas_call(
        functools.partial(_dispatch_kernel, tm=tm),
        grid=(n_tiles,),
        in_specs=[pl.BlockSpec((1, 1, tm * TOP_K), lambda i: (i, 0, 0), memory_space=pltpu.SMEM),
                  pl.BlockSpec((tm, d), lambda i: (i, 0)),
                  pl.BlockSpec(memory_space=pl.ANY)],
        out_specs=pl.BlockSpec(memory_space=pl.ANY),
        out_shape=jax.ShapeDtypeStruct((cap, d), u2.dtype),
        scratch_shapes=[pltpu.SemaphoreType.DMA(())],
        input_output_aliases={2: 0},
        compiler_params=_cparams("arbitrary"), name="dispatch",
    )(dest3, u2, xs0)


def _expert_kernel(be_ref, valid_ref, xs_ref, wgu_ref, bgu_ref, wd_ref, bd_ref, y_ref):
    b = pl.program_id(0)

    @pl.when(valid_ref[b] == 1)
    def _():
        x = xs_ref[...].astype(BF16)
        h = jnp.dot(x, wgu_ref[0], preferred_element_type=F32) + bgu_ref[0]
        hg = jnp.minimum(h[:, :D_FF], SWIGLU_LIMIT)
        hl = jnp.clip(h[:, D_FF:], -SWIGLU_LIMIT, SWIGLU_LIMIT)
        a = hg * jax.nn.sigmoid(SWIGLU_ALPHA * hg) * (hl + 1.0)
        y_ref[...] = jnp.dot(a.astype(BF16), wd_ref[0], preferred_element_type=F32) + bd_ref[0]

    @pl.when(valid_ref[b] == 0)
    def _():
        y_ref[...] = jnp.zeros_like(y_ref)


def _experts(xs, block_e, block_valid, wp):
    cap, d = xs.shape
    n_blocks = cap // EXPERT_BLOCK
    grid_spec = pltpu.PrefetchScalarGridSpec(
        num_scalar_prefetch=2,
        grid=(n_blocks,),
        in_specs=[pl.BlockSpec((EXPERT_BLOCK, d), lambda b, be, bv: (b, 0)),
                  pl.BlockSpec((1, d, 2 * D_FF), lambda b, be, bv: (be[b], 0, 0)),
                  pl.BlockSpec((1, 1, 2 * D_FF), lambda b, be, bv: (be[b], 0, 0)),
                  pl.BlockSpec((1, D_FF, d), lambda b, be, bv: (be[b], 0, 0)),
                  pl.BlockSpec((1, 1, d), lambda b, be, bv: (be[b], 0, 0))],
        out_specs=pl.BlockSpec((EXPERT_BLOCK, d), lambda b, be, bv: (b, 0)),
    )
    return pl.pallas_call(
        _expert_kernel, grid_spec=grid_spec,
        out_shape=jax.ShapeDtypeStruct((cap, d), F32),
        compiler_params=_cparams("arbitrary"), name="experts",
    )(block_e, block_valid, xs, wp["wgu"], wp["bgu"], wp["wd"], wp["bd"])


def _combine_kernel(dest_ref, x1_ref, gate_ref, mod_ref, gfin_ref, y_hbm, o_ref, buf, sem, *, tm):
    def issue(r, c):
        for k in range(TOP_K):
            d = dest_ref[0, 0, r * TOP_K + k]
            _row_copy(y_hbm.at[pl.ds(d, 1)], buf.at[k, pl.ds(r, 1)], sem).start(priority=k % 2)
        return c

    lax.fori_loop(0, tm, issue, 0, unroll=ROW_LOOP_UNROLL)

    def drain(r, c):
        for k in range(TOP_K):
            _row_copy(y_hbm.at[pl.ds(0, 1)], buf.at[k, pl.ds(0, 1)], sem).wait()
        return c

    lax.fori_loop(0, tm, drain, 0, unroll=ROW_LOOP_UNROLL)
    gate = gate_ref[...]
    y = gate[:, 0:1] * buf[0]
    for k in range(1, TOP_K):
        y = y + gate[:, k:k + 1] * buf[k]
    x2 = x1_ref[...] + mod_ref[_GT_M:_GT_M + 1, :] * y
    o_ref[...] = _rms(x2, gfin_ref[...])


def _combine(x1, gate, mod, g_final, yb, dest, seq, tm):
    n_tok, d = x1.shape
    n_tiles = n_tok // tm
    tiles_per_seq = seq // tm
    dest3 = dest.reshape(n_tiles, 1, tm * TOP_K)
    return pl.pallas_call(
        functools.partial(_combine_kernel, tm=tm),
        grid=(n_tiles,),
        in_specs=[pl.BlockSpec((1, 1, tm * TOP_K), lambda i: (i, 0, 0), memory_space=pltpu.SMEM),
                  pl.BlockSpec((tm, d), lambda i: (i, 0)),
                  pl.BlockSpec((tm, LANE), lambda i: (i, 0)),
                  pl.BlockSpec((None, MOD_ROWS, d), lambda i: (i // tiles_per_seq, 0, 0)),
                  pl.BlockSpec((1, d), lambda i: (0, 0)),
                  pl.BlockSpec(memory_space=pl.ANY)],
        out_specs=pl.BlockSpec((tm, d), lambda i: (i, 0)),
        out_shape=jax.ShapeDtypeStruct((n_tok, d), F32),
        scratch_shapes=[pltpu.VMEM((TOP_K, tm, d), F32), pltpu.SemaphoreType.DMA(())],
        compiler_params=_cparams("arbitrary"), name="combine",
    )(dest3, x1, gate, mod, g_final, yb)


def _slot_cols(w, n_heads, width, offset=0):
    k = w.shape[0]
    w3 = w.reshape(k, n_heads, width)
    out = jnp.zeros((k, n_heads, LANE), w.dtype).at[:, :, offset:offset + width].set(w3)
    return out.reshape(k, n_heads * LANE)


def _prep_weights(g_attn, g_moe, w_in, g_qn, g_kn, g_qlat, w_uq, g_kvlat, w_ukv, w_br_a, w_br_b, w_out,
                  w_router, b_router, w_gu, b_gu, w_down, b_down):
    d = D_MODEL
    wa = N_HEADS_A * HEAD_DIM_A
    wk = N_KV_A * HEAD_DIM_A
    c0 = 0
    w_qa = w_in[:, c0:c0 + wa]; c0 += wa
    w_ka = w_in[:, c0:c0 + wk]; c0 += wk
    w_va = w_in[:, c0:c0 + wk]; c0 += wk
    w_ql = w_in[:, c0:c0 + Q_LORA]; c0 += Q_LORA
    w_kvl = w_in[:, c0:c0 + KV_LORA]; c0 += KV_LORA
    w_kpe = w_in[:, c0:c0 + ROPE_DIM_B]; c0 += ROPE_DIM_B
    w_gate = w_in[:, c0:]
    w1 = jnp.concatenate([
        _slot_cols(w_qa, N_HEADS_A, HEAD_DIM_A), _slot_cols(w_ka, N_KV_A, HEAD_DIM_A), w_ql, w_kvl,
        _slot_cols(w_kpe, 1, ROPE_DIM_B, NOPE_DIM), w_gate], axis=1).astype(BF16)
    ukv = w_ukv.reshape(KV_LORA, N_HEADS_B, NOPE_DIM + V_DIM_B)
    w_uk = ukv[:, :, :NOPE_DIM].reshape(KV_LORA, N_HEADS_B * NOPE_DIM)
    w_uv = ukv[:, :, NOPE_DIM:].reshape(KV_LORA, N_HEADS_B * V_DIM_B)
    pad_gain = lambda g: jnp.zeros((1, LANE), F32).at[0, :g.shape[0]].set(g)
    return {
        "g_attn": g_attn.reshape(1, d), "g_moe": g_moe.reshape(1, d),
        "w1": w1, "wvat": w_va.T.astype(BF16),
        "g_qa": pad_gain(g_qn), "g_ka": pad_gain(g_kn),
        "g_qlat": g_qlat.reshape(1, Q_LORA), "g_kvlat": g_kvlat.reshape(1, KV_LORA),
        "wuq": _slot_cols(w_uq, N_HEADS_B, NOPE_DIM + ROPE_DIM_B).astype(BF16),
        "wuk": _slot_cols(w_uk, N_HEADS_B, NOPE_DIM).astype(BF16),
        "wuvt": w_uv.T.astype(BF16),
        "wbra": w_br_a.astype(BF16), "wbrb": w_br_b.astype(BF16), "wout": w_out.astype(BF16),
        "wr": jnp.zeros((d, LANE), F32).at[:, :N_EXPERTS].set(w_router).astype(BF16),
        "br": jnp.full((1, LANE), NEG_BIG, F32).at[0, :N_EXPERTS].set(b_router),
        "wgu": w_gu.astype(BF16), "bgu": b_gu.reshape(N_EXPERTS, 1, 2 * D_FF),
        "wd": w_down.astype(BF16), "bd": b_down.reshape(N_EXPERTS, 1, d),
    }


def _rope_tables(seq, rot_dim, offset):
    half = rot_dim // 2
    quarter = half // 2
    t = jnp.arange(seq, dtype=jnp.int32)
    row = (t // GRID_W).astype(F32)
    col = (t % GRID_W).astype(F32)
    inv = ROPE_THETA ** (-jnp.arange(0, half, 2, dtype=F32) / half)
    ar, ac = row[:, None] * inv, col[:, None] * inv
    z = jnp.zeros((seq, quarter), F32)
    cos = jnp.concatenate([jnp.cos(ar), jnp.cos(ar), jnp.cos(ac), jnp.cos(ac)], axis=1)
    sinm = jnp.concatenate([-jnp.sin(ar), z, -jnp.sin(ac), z], axis=1)
    sinp = jnp.concatenate([z, jnp.sin(ar), z, jnp.sin(ac)], axis=1)

    def place(a, fill):
        return jnp.full((seq, LANE), fill, F32).at[:, offset:offset + rot_dim].set(a)

    return jnp.stack([place(cos, 1.0), place(sinm, 0.0), place(sinp, 0.0)])


def _trunk(x, mod, wp, g_final):
    n_batch, seq, d = x.shape
    n_tok = n_batch * seq
    x2d = x.reshape(n_tok, d)
    tm = min(256, seq)
    ta = _rope_tables(seq, HEAD_DIM_A, 0)
    tb = _rope_tables(seq, ROPE_DIM_B, NOPE_DIM)

    qa, ka, vta, qb, kb, vtb, sig = _pre(x2d, mod, wp, ta, tb, seq, tm)
    ota = _attention(qa, ka, vta, n_batch=n_batch, seq=seq, n_groups=N_KV_A, n_sub=GROUP_A,
                     tq=min(128, seq), dv=HEAD_DIM_A)
    otb = _attention(qb, kb, vtb, n_batch=n_batch, seq=seq, n_groups=N_HEADS_B, n_sub=1,
                     tq=min(512, seq), dv=V_DIM_B)
    x1, u2, eid, rank, gate, cnt = _post(x2d, mod, ota, otb, sig, wp, seq, tm)

    n_assign = n_tok * TOP_K
    n_blocks = -(-n_assign // EXPERT_BLOCK) + N_EXPERTS
    cap = n_blocks * EXPERT_BLOCK
    counts = cnt[0, :N_EXPERTS].astype(jnp.int32)
    padded = (counts + EXPERT_BLOCK - 1) // EXPERT_BLOCK * EXPERT_BLOCK
    pad_end = jnp.cumsum(padded)
    pad_start = pad_end - padded
    dest = pad_start[eid[:, :TOP_K]] + rank[:, :TOP_K]
    block_lo = jnp.arange(n_blocks, dtype=jnp.int32) * EXPERT_BLOCK
    block_e = jnp.minimum(jnp.sum(block_lo[:, None] >= pad_end[None, :], axis=1), N_EXPERTS - 1).astype(jnp.int32)
    block_valid = (block_lo < pad_end[-1]).astype(jnp.int32)

    xs = _dispatch(u2, dest, cap, tm)
    yb = _experts(xs, block_e, block_valid, wp)
    out = _combine(x1, gate, mod, g_final.reshape(1, d), yb, dest, seq, min(128, seq))
    return out.reshape(n_batch, seq, d)


def kernel(x_prompt, x_sample, c_prompt, c_sample, w_mod, b_mod, g_attn, g_moe, w_in, g_qn, g_kn, g_qlat,
           w_uq, g_kvlat, w_ukv, w_br_a, w_br_b, w_out, w_router, b_router, w_gu, b_gu, w_down, b_down,
           g_final):
    assert w_mod.shape[0] == 1, "single-layer trunk"
    wp = _prep_weights(g_attn[0], g_moe[0], w_in[0], g_qn[0], g_kn[0], g_qlat[0], w_uq[0], g_kvlat[0],
                       w_ukv[0], w_br_a[0], w_br_b[0], w_out[0], w_router[0], b_router[0], w_gu[0],
                       b_gu[0], w_down[0], b_down[0])
    nb_p = c_prompt.shape[0]
    mod = _modulation(jnp.concatenate([c_prompt, c_sample], axis=0), w_mod[0], b_mod[0])
    y_prompt = _trunk(x_prompt, mod[:nb_p], wp, g_final)
    y_sample = _trunk(x_sample, mod[nb_p:], wp, g_final)
    return (y_prompt, y_sample)
```

```python
import functools

import jax
import jax.numpy as jnp
from jax import lax
from jax.experimental import pallas as pl
from jax.experimental.pallas import tpu as pltpu

F32 = jnp.float32
BF16 = jnp.bfloat16

D_MODEL = 1024
GRID_W = 64
ROPE_THETA = 10000.0
EPS = 1e-6
N_HEADS_A = 8
N_KV_A = 2
GROUP_A = N_HEADS_A // N_KV_A
HEAD_DIM_A = 64
N_HEADS_B = 8
Q_LORA = 768
KV_LORA = 256
NOPE_DIM = 64
ROPE_DIM_B = 32
V_DIM_B = 64
N_EXPERTS = 32
TOP_K = 4
D_FF = D_MODEL
SWIGLU_ALPHA = 1.702
SWIGLU_LIMIT = 7.0
EXPERT_BLOCK = 256

LANE = 128
BF16_SUBLANES = 16
NEG_BIG = -1e30
LOG2E = 1.4426950408889634
ONES_ROWS = BF16_SUBLANES
VMEM_LIMIT = 52 * 1024 * 1024
ATTN_FLAGS = None
ATTN_LAG = 2
ATTN_RING = 4
ATTN_COLS = 1024
ATTN_UNROLL = 16
ROW_LOOP_UNROLL = 4
BASE_LANE = LANE - 1
DEN_MIN, DEN_MAX = 1e-30, 1e30

_C_QA = 0
_C_KA = _C_QA + N_HEADS_A * LANE
_C_QLAT = _C_KA + N_KV_A * LANE
_C_KVLAT = _C_QLAT + Q_LORA
_C_KPE = _C_KVLAT + KV_LORA
_C_GATE = _C_KPE + LANE
_C_END = _C_GATE + 2 * D_MODEL

_SH_A, _SC_A, _GT_A, _SH_M, _SC_M, _GT_M = range(6)
MOD_ROWS = 8

_NT = (((1,), (1,)), ((), ()))
_TN = (((0,), (0,)), ((), ()))


def _cparams(*sem, flags=None):
    return pltpu.CompilerParams(dimension_semantics=sem, vmem_limit_bytes=VMEM_LIMIT, flags=flags)


def _rms(x, g):
    return x * lax.rsqrt(jnp.mean(x * x, axis=-1, keepdims=True) + EPS) * g


def _mod_kernel(c_ref, w_ref, b_ref, o_ref):
    c = c_ref[...]
    s = c * jax.nn.sigmoid(c)
    o_ref[...] = jnp.dot(s.astype(BF16), w_ref[...].astype(BF16), preferred_element_type=F32) + b_ref[...]


def _modulation(c, w_mod, b_mod):
    nb, d = c.shape
    rows = -(-nb // 8) * 8
    cp = jnp.zeros((rows, d), F32).at[:nb].set(c)
    n_out = w_mod.shape[1]
    tn = 512
    out = pl.pallas_call(
        _mod_kernel,
        grid=(n_out // tn,),
        in_specs=[pl.BlockSpec((rows, d), lambda j: (0, 0)),
                  pl.BlockSpec((d, tn), lambda j: (0, j)),
                  pl.BlockSpec((1, tn), lambda j: (0, j))],
        out_specs=pl.BlockSpec((rows, tn), lambda j: (0, j)),
        out_shape=jax.ShapeDtypeStruct((rows, n_out), F32),
        compiler_params=_cparams("arbitrary"),
        name="mod",
    )(cp, w_mod, b_mod.reshape(1, n_out))
    mod = out[:nb].reshape(nb, 6, d)
    return jnp.concatenate([mod, jnp.zeros((nb, MOD_ROWS - 6, d), F32)], axis=1)


def _rope(x, cos, sinm, sinp, shift):
    return x * cos + pltpu.roll(x, LANE - shift, 1) * sinm + pltpu.roll(x, shift, 1) * sinp


def _pre_kernel(x_ref, mod_ref, gattn_ref, w1_ref, wvat_ref, gqa_ref, gka_ref, gqlat_ref, wuq_ref,
                gkvlat_ref, wuk_ref, wuvt_ref, ta_ref, tb_ref,
                qa_ref, ka_ref, vta_ref, qb_ref, kb_ref, vtb_ref, sig_ref):
    x = x_ref[...]
    u = _rms(x, gattn_ref[...]) * (1.0 + mod_ref[_SC_A:_SC_A + 1, :]) + mod_ref[_SH_A:_SH_A + 1, :]
    ub = u.astype(BF16)

    def proj(lo, hi):
        return jnp.dot(ub, w1_ref[:, lo:hi], preferred_element_type=F32)

    cos_a, sinm_a, sinp_a = ta_ref[0], ta_ref[1], ta_ref[2]
    cos_b, sinm_b, sinp_b = tb_ref[0], tb_ref[1], tb_ref[2]
    shift_a = HEAD_DIM_A // 4
    shift_b = ROPE_DIM_B // 4

    def head_a(xs, g, scale):
        ms = jnp.sum(xs * xs, axis=-1, keepdims=True) * (1.0 / HEAD_DIM_A)
        xn = xs * lax.rsqrt(ms + EPS) * g
        return (_rope(xn, cos_a, sinm_a, sinp_a, shift_a) * scale).astype(BF16)

    qa = proj(_C_QA, _C_KA)
    for h in range(N_HEADS_A):
        qa_ref[:, h * LANE:(h + 1) * LANE] = head_a(qa[:, h * LANE:(h + 1) * LANE], gqa_ref[...],
                                                    HEAD_DIM_A ** -0.5 * LOG2E)
    ka = proj(_C_KA, _C_QLAT)
    base_lane = lax.broadcasted_iota(jnp.int32, (x.shape[0], LANE), 1) == BASE_LANE
    for h in range(N_KV_A):
        kh = head_a(ka[:, h * LANE:(h + 1) * LANE], gka_ref[...], 1.0)
        ka_ref[:, h * LANE:(h + 1) * LANE] = jnp.where(base_lane, 1.0, kh).astype(BF16)
    vta = lax.dot_general(wvat_ref[...], ub, _NT, preferred_element_type=F32)
    vta_ref[0, :, :HEAD_DIM_A, :] = vta.astype(BF16).reshape(N_KV_A, HEAD_DIM_A, vta.shape[-1])
    vta_ref[0, :, HEAD_DIM_A:, :] = jnp.ones((N_KV_A, ONES_ROWS, vta.shape[-1]), BF16)

    qn = _rms(proj(_C_QLAT, _C_KVLAT), gqlat_ref[...]).astype(BF16)
    qb = jnp.dot(qn, wuq_ref[...], preferred_element_type=F32)
    scale_b = (NOPE_DIM + ROPE_DIM_B) ** -0.5 * LOG2E
    for h in range(N_HEADS_B):
        sl = slice(h * LANE, (h + 1) * LANE)
        qb_ref[:, sl] = (_rope(qb[:, sl], cos_b, sinm_b, sinp_b, shift_b) * scale_b).astype(BF16)
    kvn = _rms(proj(_C_KVLAT, _C_KPE), gkvlat_ref[...]).astype(BF16)
    kpe = _rope(proj(_C_KPE, _C_GATE), cos_b, sinm_b, sinp_b, shift_b)
    kb = jnp.dot(kvn, wuk_ref[...], preferred_element_type=F32)
    for h in range(N_HEADS_B):
        sl = slice(h * LANE, (h + 1) * LANE)
        kb_ref[:, sl] = jnp.where(base_lane, 1.0, kb[:, sl] + kpe).astype(BF16)
    vtb = lax.dot_general(wuvt_ref[...], kvn, _NT, preferred_element_type=F32)
    vtb_ref[0, :, :V_DIM_B, :] = vtb.astype(BF16).reshape(N_HEADS_B, V_DIM_B, vtb.shape[-1])
    vtb_ref[0, :, V_DIM_B:, :] = jnp.ones((N_HEADS_B, ONES_ROWS, vtb.shape[-1]), BF16)

    sig_ref[...] = jax.nn.sigmoid(proj(_C_GATE, _C_END)).astype(BF16)


def _pre(x2d, mod, wp, tables_a, tables_b, seq, tm):
    n_tok, d = x2d.shape
    n_tiles = n_tok // tm
    tiles_per_seq = seq // tm
    full = lambda a: pl.BlockSpec(a.shape, lambda i: (0,) * a.ndim)
    tok = lambda cols: pl.BlockSpec((tm, cols), lambda i: (i, 0))
    tab = pl.BlockSpec((3, tm, LANE), lambda i: (0, i % tiles_per_seq, 0))
    in_specs = [tok(d),
                pl.BlockSpec((None, MOD_ROWS, d), lambda i: (i // tiles_per_seq, 0, 0)),
                full(wp["g_attn"]), full(wp["w1"]), full(wp["wvat"]), full(wp["g_qa"]), full(wp["g_ka"]),
                full(wp["g_qlat"]), full(wp["wuq"]), full(wp["g_kvlat"]), full(wp["wuk"]), full(wp["wuvt"]),
                tab, tab]
    out_shape = [
        jax.ShapeDtypeStruct((n_tok, N_HEADS_A * LANE), BF16),
        jax.ShapeDtypeStruct((n_tok, N_KV_A * LANE), BF16),
        jax.ShapeDtypeStruct((n_tiles, N_KV_A, HEAD_DIM_A + ONES_ROWS, tm), BF16),
        jax.ShapeDtypeStruct((n_tok, N_HEADS_B * LANE), BF16),
        jax.ShapeDtypeStruct((n_tok, N_HEADS_B * LANE), BF16),
        jax.ShapeDtypeStruct((n_tiles, N_HEADS_B, V_DIM_B + ONES_ROWS, tm), BF16),
        jax.ShapeDtypeStruct((n_tok, 2 * D_MODEL), BF16),
    ]
    out_specs = [tok(N_HEADS_A * LANE), tok(N_KV_A * LANE),
                 pl.BlockSpec((1, N_KV_A, HEAD_DIM_A + ONES_ROWS, tm), lambda i: (i, 0, 0, 0)),
                 tok(N_HEADS_B * LANE), tok(N_HEADS_B * LANE),
                 pl.BlockSpec((1, N_HEADS_B, V_DIM_B + ONES_ROWS, tm), lambda i: (i, 0, 0, 0)),
                 tok(2 * D_MODEL)]
    return pl.pallas_call(
        _pre_kernel, grid=(n_tiles,), in_specs=in_specs, out_specs=out_specs, out_shape=out_shape,
        compiler_params=_cparams("arbitrary"), name="pre",
    )(x2d, mod, wp["g_attn"], wp["w1"], wp["wvat"], wp["g_qa"], wp["g_ka"], wp["g_qlat"], wp["wuq"],
      wp["g_kvlat"], wp["wuk"], wp["wuvt"], tables_a, tables_b)


def _attn_kernel(q_ref, k_ref, vt_ref, o_ref, kmax_ref, *p_bufs, n_sub, tq, tk, n_chunks, dv, unroll):
    lag = ATTN_LAG
    cols = n_sub * tq
    if n_sub == 1:
        q = q_ref[...]
    else:
        q = jnp.concatenate([q_ref[:, j * LANE:(j + 1) * LANE] for j in range(n_sub)], axis=0)

    def key_chunk(i):
        return k_ref[pl.ds(pl.multiple_of(i * tk, tk), tk), :]

    @pl.when(pl.program_id(2) == 0)
    def _():
        def norm_body(c, mx):
            kc = key_chunk(c).astype(F32)
            n2 = jnp.sum(kc * kc, axis=-1, keepdims=True)
            return jnp.maximum(mx, jnp.max(n2, axis=0, keepdims=True))

        kmax2 = lax.fori_loop(0, n_chunks, norm_body, jnp.zeros((1, 1), F32))
        kmax_ref[...] = jnp.broadcast_to(jnp.sqrt(jnp.maximum(kmax2 - 1.0, 0.0)), kmax_ref.shape)

    qt = q.astype(F32).T
    base = jnp.sqrt(jnp.sum(qt * qt, axis=0, keepdims=True)) * kmax_ref[0:1, 0:1]
    row = lax.broadcasted_iota(jnp.int32, qt.shape, 0)
    qt_aug = jnp.where(row == BASE_LANE, -base, qt).astype(BF16)

    def pv(c, p_ref):
        return jnp.dot(vt_ref[c], p_ref[...], preferred_element_type=F32)

    def trip(i, u, acc):
        acc = acc + pv(jnp.maximum(i - lag, 0), p_bufs[(u - lag) % ATTN_RING])
        s = jnp.dot(key_chunk(i), qt_aug, preferred_element_type=F32)
        p_bufs[u % ATTN_RING][...] = jnp.exp2(s).astype(BF16)
        return acc

    def body(j, acc):
        for u in range(unroll):
            acc = trip(j * unroll + u, u, acc)
        return acc

    for c in range(lag):
        p_bufs[(-1 - c) % ATTN_RING][...] = jnp.zeros((tk, cols), BF16)
    acc = lax.fori_loop(0, n_chunks // unroll, body, jnp.zeros((dv + ONES_ROWS, cols), F32))
    for c in range(n_chunks - lag, n_chunks):
        acc = acc + pv(c, p_bufs[c % ATTN_RING])

    def emit(acc):
        o = acc[:dv] / acc[dv:dv + 1]
        for j in range(n_sub):
            o_ref[j * dv:(j + 1) * dv, :] = o[:, j * tq:(j + 1) * tq].astype(BF16)

    emit(acc)
    den = acc[dv:dv + 1]
    safe = jnp.logical_and(den > DEN_MIN, den < DEN_MAX)
    n_unsafe = jnp.sum(jnp.where(safe, 0.0, 1.0))

    @pl.when(n_unsafe > 0.0)
    def _():
        def exact_body(i, carry):
            m, acc = carry
            s = lax.dot_general(key_chunk(i), q, _NT, preferred_element_type=F32)
            m_new = jnp.maximum(m, jnp.max(s, axis=0, keepdims=True))
            p = jnp.exp2(s - m_new).astype(BF16)
            return m_new, jnp.exp2(m - m_new) * acc + jnp.dot(vt_ref[i], p, preferred_element_type=F32)

        init = (jnp.full((1, cols), NEG_BIG, F32), jnp.zeros((dv + ONES_ROWS, cols), F32))
        emit(lax.fori_loop(0, n_chunks, exact_body, init)[1])


def _attention(q, k, vt, *, n_batch, seq, n_groups, n_sub, tq, dv):
    n_tok = q.shape[0]
    tk = vt.shape[-1]
    n_chunks = seq // tk
    q_tiles = seq // tq
    unroll = ATTN_UNROLL
    while n_chunks % unroll:
        unroll //= 2
    assert unroll % ATTN_RING == 0, "buffer slots must be static per trip"
    kern = functools.partial(_attn_kernel, n_sub=n_sub, tq=tq, tk=tk, n_chunks=n_chunks, dv=dv,
                             unroll=unroll)
    cols = n_sub * tq
    return pl.pallas_call(
        kern,
        grid=(n_batch, n_groups, q_tiles),
        in_specs=[pl.BlockSpec((tq, n_sub * LANE), lambda b, g, i: (b * q_tiles + i, g)),
                  pl.BlockSpec((seq, LANE), lambda b, g, i: (b, g)),
                  pl.BlockSpec((n_chunks, None, dv + ONES_ROWS, tk), lambda b, g, i: (b, g, 0, 0))],
        out_specs=pl.BlockSpec((n_sub * dv, tq), lambda b, g, i: (g, b * q_tiles + i)),
        out_shape=jax.ShapeDtypeStruct((n_groups * n_sub * dv, n_tok), BF16),
        scratch_shapes=[pltpu.VMEM((8, LANE), F32)] + [pltpu.VMEM((tk, cols), BF16)] * ATTN_RING,
        compiler_params=_cparams("arbitrary", "arbitrary", "arbitrary", flags=ATTN_FLAGS),
        name="attn",
    )(q, k, vt)


def _post_kernel(x_ref, mod_ref, ota_ref, otb_ref, sig_ref, wbra_ref, wbrb_ref, wout_ref, gmoe_ref,
                 wr_ref, br_ref,
                 x1_ref, u2_ref, eid_ref, rank_ref, gate_ref, cnt_ref, carry_ref):
    i = pl.program_id(0)

    @pl.when(i == 0)
    def _():
        carry_ref[...] = jnp.zeros_like(carry_ref)

    d = D_MODEL
    ya = lax.dot_general(ota_ref[...], wbra_ref[...], _TN, preferred_element_type=F32)
    yb = lax.dot_general(otb_ref[...], wbrb_ref[...], _TN, preferred_element_type=F32)
    mixed = sig_ref[:, :d].astype(F32) * ya + sig_ref[:, d:].astype(F32) * yb
    att = jnp.dot(mixed.astype(BF16), wout_ref[...], preferred_element_type=F32)
    x1 = x_ref[...] + mod_ref[_GT_A:_GT_A + 1, :] * att
    x1_ref[...] = x1
    u2 = _rms(x1, gmoe_ref[...]) * (1.0 + mod_ref[_SC_M:_SC_M + 1, :]) + mod_ref[_SH_M:_SH_M + 1, :]
    u2_ref[...] = u2

    logits = jnp.dot(u2.astype(BF16), wr_ref[...], preferred_element_type=F32) + br_ref[...]
    tm = logits.shape[0]
    lane = lax.broadcasted_iota(jnp.int32, (tm, LANE), 1)
    vals = logits
    top_v, top_i = [], []
    for _ in range(TOP_K):
        mx = jnp.max(vals, axis=-1, keepdims=True)
        idx = jnp.min(jnp.where(vals == mx, lane, LANE), axis=-1, keepdims=True)
        top_v.append(mx)
        top_i.append(idx)
        vals = jnp.where(lane == idx, -jnp.inf, vals)
    ex = [jnp.exp(v - top_v[0]) for v in top_v]
    den = ex[0] + ex[1] + ex[2] + ex[3]

    onehot = [lane == idx for idx in top_i]
    cnt = sum(oh.astype(F32) for oh in onehot)
    r_i = lax.broadcasted_iota(jnp.int32, (tm, tm), 0)
    c_i = lax.broadcasted_iota(jnp.int32, (tm, tm), 1)
    lower = (c_i < r_i).astype(BF16)
    before = jnp.dot(lower, cnt.astype(BF16), preferred_element_type=F32) + carry_ref[...]
    eid = jnp.zeros((tm, LANE), jnp.int32)
    rank = jnp.zeros((tm, LANE), jnp.int32)
    gate = jnp.zeros((tm, LANE), F32)
    for k in range(TOP_K):
        rk = jnp.sum(jnp.where(onehot[k], before, 0.0), axis=-1, keepdims=True).astype(jnp.int32)
        eid = jnp.where(lane == k, top_i[k], eid)
        rank = jnp.where(lane == k, rk, rank)
        gate = jnp.where(lane == k, ex[k] / den, gate)
    eid_ref[...] = eid
    rank_ref[...] = rank
    gate_ref[...] = gate
    carry = carry_ref[...] + jnp.sum(cnt, axis=0, keepdims=True)
    carry_ref[...] = carry
    cnt_ref[...] = carry


def _post(x2d, mod, ota, otb, sig, wp, seq, tm):
    n_tok, d = x2d.shape
    n_tiles = n_tok // tm
    tiles_per_seq = seq // tm
    full = lambda a: pl.BlockSpec(a.shape, lambda i: (0,) * a.ndim)
    tok = lambda cols: pl.BlockSpec((tm, cols), lambda i: (i, 0))
    tcol = lambda rows: pl.BlockSpec((rows, tm), lambda i: (0, i))
    in_specs = [tok(d), pl.BlockSpec((None, MOD_ROWS, d), lambda i: (i // tiles_per_seq, 0, 0)),
                tcol(ota.shape[0]), tcol(otb.shape[0]), tok(2 * d),
                full(wp["wbra"]), full(wp["wbrb"]), full(wp["wout"]), full(wp["g_moe"]),
                full(wp["wr"]), full(wp["br"])]
    out_shape = [jax.ShapeDtypeStruct((n_tok, d), F32), jax.ShapeDtypeStruct((n_tok, d), F32),
                 jax.ShapeDtypeStruct((n_tok, LANE), jnp.int32), jax.ShapeDtypeStruct((n_tok, LANE), jnp.int32),
                 jax.ShapeDtypeStruct((n_tok, LANE), F32), jax.ShapeDtypeStruct((1, LANE), F32)]
    out_specs = [tok(d), tok(d), tok(LANE), tok(LANE), tok(LANE), pl.BlockSpec((1, LANE), lambda i: (0, 0))]
    return pl.pallas_call(
        _post_kernel, grid=(n_tiles,), in_specs=in_specs, out_specs=out_specs, out_shape=out_shape,
        scratch_shapes=[pltpu.VMEM((1, LANE), F32)],
        compiler_params=_cparams("arbitrary"), name="post",
    )(x2d, mod, ota, otb, sig, wp["wbra"], wp["wbrb"], wp["wout"], wp["g_moe"], wp["wr"], wp["br"])


def _row_copy(src, dst, sem):
    return pltpu.make_async_copy(src, dst, sem)


def _dispatch_kernel(pad_end_ref, dest_ref, u_ref, xs_ref, zero_ref, sem, zero_sem, *, tm):
    @pl.when(pl.program_id(0) == 0)
    def _():
        zero_ref[...] = jnp.zeros_like(zero_ref)

        def tail_copy(e):
            start = pl.multiple_of(pad_end_ref[e] - EXPERT_BLOCK, EXPERT_BLOCK)
            return pltpu.make_async_copy(zero_ref, xs_ref.at[pl.ds(start, EXPERT_BLOCK)], zero_sem)

        def nonempty(e):
            return pad_end_ref[e] > (pad_end_ref[e - 1] if e else 0)

        for e in range(N_EXPERTS):
            pl.when(nonempty(e))(lambda e=e: tail_copy(e).start())
        for e in range(N_EXPERTS):
            pl.when(nonempty(e))(lambda e=e: tail_copy(e).wait())

        def unused_copy(b):
            start = pl.multiple_of(b * EXPERT_BLOCK, EXPERT_BLOCK)
            return pltpu.make_async_copy(zero_ref, xs_ref.at[pl.ds(start, EXPERT_BLOCK)], zero_sem)

        first_unused = pad_end_ref[N_EXPERTS - 1] // EXPERT_BLOCK
        n_blocks = xs_ref.shape[0] // EXPERT_BLOCK
        lax.fori_loop(first_unused, n_blocks, lambda b, c: (unused_copy(b).start(), c)[1], 0)
        lax.fori_loop(first_unused, n_blocks, lambda b, c: (unused_copy(b).wait(), c)[1], 0)

    def issue(r, c):
        for k in range(TOP_K):
            d = dest_ref[0, 0, r * TOP_K + k]
            _row_copy(u_ref.at[pl.ds(r, 1)], xs_ref.at[pl.ds(d, 1)], sem).start(priority=k % 2)
        return c

    lax.fori_loop(0, tm, issue, 0, unroll=ROW_LOOP_UNROLL)

    def drain(r, c):
        for k in range(TOP_K):
            _row_copy(u_ref.at[pl.ds(0, 1)], xs_ref.at[pl.ds(0, 1)], sem).wait()
        return c

    lax.fori_loop(0, tm, drain, 0, unroll=ROW_LOOP_UNROLL)


def _dispatch(u2, dest, pad_end, cap, tm):
    n_tok, d = u2.shape
    n_tiles = n_tok // tm
    dest3 = dest.reshape(n_tiles, 1, tm * TOP_K)
    grid_spec = pltpu.PrefetchScalarGridSpec(
        num_scalar_prefetch=1,
        grid=(n_tiles,),
        in_specs=[pl.BlockSpec((1, 1, tm * TOP_K), lambda i, pe: (i, 0, 0), memory_space=pltpu.SMEM),
                  pl.BlockSpec((tm, d), lambda i, pe: (i, 0))],
        out_specs=pl.BlockSpec(memory_space=pl.ANY),
        scratch_shapes=[pltpu.VMEM((EXPERT_BLOCK, d), u2.dtype), pltpu.SemaphoreType.DMA(()),
                        pltpu.SemaphoreType.DMA(())],
    )
    return pl.pallas_call(
        functools.partial(_dispatch_kernel, tm=tm), grid_spec=grid_spec,
        out_shape=jax.ShapeDtypeStruct((cap, d), u2.dtype),
        compiler_params=_cparams("arbitrary"), name="dispatch",
    )(pad_end, dest3, u2)


def _expert_kernel(be_ref, valid_ref, xs_ref, wgu_ref, bgu_ref, wd_ref, bd_ref, y_ref):
    b = pl.program_id(0)

    @pl.when(valid_ref[b] == 1)
    def _():
        x = xs_ref[...].astype(BF16)
        h = jnp.dot(x, wgu_ref[0], preferred_element_type=F32) + bgu_ref[0]
        hg = jnp.minimum(h[:, :D_FF], SWIGLU_LIMIT)
        hl = jnp.clip(h[:, D_FF:], -SWIGLU_LIMIT, SWIGLU_LIMIT)
        a = hg * jax.nn.sigmoid(SWIGLU_ALPHA * hg) * (hl + 1.0)
        y_ref[...] = jnp.dot(a.astype(BF16), wd_ref[0], preferred_element_type=F32) + bd_ref[0]

    @pl.when(valid_ref[b] == 0)
    def _():
        y_ref[...] = jnp.zeros_like(y_ref)


def _experts(xs, block_e, block_valid, wp):
    cap, d = xs.shape
    n_blocks = cap // EXPERT_BLOCK
    grid_spec = pltpu.PrefetchScalarGridSpec(
        num_scalar_prefetch=2,
        grid=(n_blocks,),
        in_specs=[pl.BlockSpec((EXPERT_BLOCK, d), lambda b, be, bv: (jnp.where(bv[b] == 1, b, 0), 0)),
                  pl.BlockSpec((1, d, 2 * D_FF), lambda b, be, bv: (be[b], 0, 0)),
                  pl.BlockSpec((1, 1, 2 * D_FF), lambda b, be, bv: (be[b], 0, 0)),
                  pl.BlockSpec((1, D_FF, d), lambda b, be, bv: (be[b], 0, 0)),
                  pl.BlockSpec((1, 1, d), lambda b, be, bv: (be[b], 0, 0))],
        out_specs=pl.BlockSpec((EXPERT_BLOCK, d), lambda b, be, bv: (b, 0)),
    )
    return pl.pallas_call(
        _expert_kernel, grid_spec=grid_spec,
        out_shape=jax.ShapeDtypeStruct((cap, d), F32),
        compiler_params=_cparams("arbitrary"), name="experts",
    )(block_e, block_valid, xs, wp["wgu"], wp["bgu"], wp["wd"], wp["bd"])


def _combine_kernel(dest_ref, dest_next_ref, x1_ref, gate_ref, mod_ref, gfin_ref, y_hbm, o_ref, buf, sem, *, tm):
    i = pl.program_id(0)
    slot = i % 2

    def gather(dref, s):
        def issue(r, c):
            for k in range(TOP_K):
                d = dref[0, 0, r * TOP_K + k]
                _row_copy(y_hbm.at[pl.ds(d, 1)], buf.at[s, k, pl.ds(r, 1)], sem.at[s]).start(priority=k % 2)
            return c

        lax.fori_loop(0, tm, issue, 0, unroll=ROW_LOOP_UNROLL)

    @pl.when(i == 0)
    def _():
        gather(dest_ref, slot)

    @pl.when(i + 1 < pl.num_programs(0))
    def _():
        gather(dest_next_ref, 1 - slot)

    def drain(r, c):
        for k in range(TOP_K):
            _row_copy(y_hbm.at[pl.ds(0, 1)], buf.at[slot, k, pl.ds(0, 1)], sem.at[slot]).wait()
        return c

    lax.fori_loop(0, tm, drain, 0, unroll=ROW_LOOP_UNROLL)
    gate = gate_ref[...]
    y = gate[:, 0:1] * buf[slot, 0]
    for k in range(1, TOP_K):
        y = y + gate[:, k:k + 1] * buf[slot, k]
    x2 = x1_ref[...] + mod_ref[_GT_M:_GT_M + 1, :] * y
    o_ref[...] = _rms(x2, gfin_ref[...])


def _combine(x1, gate, mod, g_final, yb, dest, seq, tm):
    n_tok, d = x1.shape
    n_tiles = n_tok // tm
    tiles_per_seq = seq // tm
    dest3 = dest.reshape(n_tiles, 1, tm * TOP_K)
    return pl.pallas_call(
        functools.partial(_combine_kernel, tm=tm),
        grid=(n_tiles,),
        in_specs=[pl.BlockSpec((1, 1, tm * TOP_K), lambda i: (i, 0, 0), memory_space=pltpu.SMEM),
                  pl.BlockSpec((1, 1, tm * TOP_K), lambda i: (jnp.minimum(i + 1, n_tiles - 1), 0, 0),
                               memory_space=pltpu.SMEM),
                  pl.BlockSpec((tm, d), lambda i: (i, 0)),
                  pl.BlockSpec((tm, LANE), lambda i: (i, 0)),
                  pl.BlockSpec((None, MOD_ROWS, d), lambda i: (i // tiles_per_seq, 0, 0)),
                  pl.BlockSpec((1, d), lambda i: (0, 0)),
                  pl.BlockSpec(memory_space=pl.ANY)],
        out_specs=pl.BlockSpec((tm, d), lambda i: (i, 0)),
        out_shape=jax.ShapeDtypeStruct((n_tok, d), F32),
        scratch_shapes=[pltpu.VMEM((2, TOP_K, tm, d), F32), pltpu.SemaphoreType.DMA((2,))],
        compiler_params=_cparams("arbitrary"), name="combine",
    )(dest3, dest3, x1, gate, mod, g_final, yb)


def _slot_cols(w, n_heads, width, offset=0):
    k = w.shape[0]
    w3 = w.reshape(k, n_heads, width)
    out = jnp.zeros((k, n_heads, LANE), w.dtype).at[:, :, offset:offset + width].set(w3)
    return out.reshape(k, n_heads * LANE)


def _prep_weights(g_attn, g_moe, w_in, g_qn, g_kn, g_qlat, w_uq, g_kvlat, w_ukv, w_br_a, w_br_b, w_out,
                  w_router, b_router, w_gu, b_gu, w_down, b_down):
    d = D_MODEL
    wa = N_HEADS_A * HEAD_DIM_A
    wk = N_KV_A * HEAD_DIM_A
    c0 = 0
    w_qa = w_in[:, c0:c0 + wa]; c0 += wa
    w_ka = w_in[:, c0:c0 + wk]; c0 += wk
    w_va = w_in[:, c0:c0 + wk]; c0 += wk
    w_ql = w_in[:, c0:c0 + Q_LORA]; c0 += Q_LORA
    w_kvl = w_in[:, c0:c0 + KV_LORA]; c0 += KV_LORA
    w_kpe = w_in[:, c0:c0 + ROPE_DIM_B]; c0 += ROPE_DIM_B
    w_gate = w_in[:, c0:]
    w1 = jnp.concatenate([
        _slot_cols(w_qa, N_HEADS_A, HEAD_DIM_A), _slot_cols(w_ka, N_KV_A, HEAD_DIM_A), w_ql, w_kvl,
        _slot_cols(w_kpe, 1, ROPE_DIM_B, NOPE_DIM), w_gate], axis=1).astype(BF16)
    ukv = w_ukv.reshape(KV_LORA, N_HEADS_B, NOPE_DIM + V_DIM_B)
    w_uk = ukv[:, :, :NOPE_DIM].reshape(KV_LORA, N_HEADS_B * NOPE_DIM)
    w_uv = ukv[:, :, NOPE_DIM:].reshape(KV_LORA, N_HEADS_B * V_DIM_B)
    pad_gain = lambda g: jnp.zeros((1, LANE), F32).at[0, :g.shape[0]].set(g)
    return {
        "g_attn": g_attn.reshape(1, d), "g_moe": g_moe.reshape(1, d),
        "w1": w1, "wvat": w_va.T.astype(BF16),
        "g_qa": pad_gain(g_qn), "g_ka": pad_gain(g_kn),
        "g_qlat": g_qlat.reshape(1, Q_LORA), "g_kvlat": g_kvlat.reshape(1, KV_LORA),
        "wuq": _slot_cols(w_uq, N_HEADS_B, NOPE_DIM + ROPE_DIM_B).astype(BF16),
        "wuk": _slot_cols(w_uk, N_HEADS_B, NOPE_DIM).astype(BF16),
        "wuvt": w_uv.T.astype(BF16),
        "wbra": w_br_a.astype(BF16), "wbrb": w_br_b.astype(BF16), "wout": w_out.astype(BF16),
        "wr": jnp.zeros((d, LANE), F32).at[:, :N_EXPERTS].set(w_router).astype(BF16),
        "br": jnp.full((1, LANE), NEG_BIG, F32).at[0, :N_EXPERTS].set(b_router),
        "wgu": w_gu.astype(BF16), "bgu": b_gu.reshape(N_EXPERTS, 1, 2 * D_FF),
        "wd": w_down.astype(BF16), "bd": b_down.reshape(N_EXPERTS, 1, d),
    }


def _rope_tables(seq, rot_dim, offset):
    half = rot_dim // 2
    quarter = half // 2
    t = jnp.arange(seq, dtype=jnp.int32)
    row = (t // GRID_W).astype(F32)
    col = (t % GRID_W).astype(F32)
    inv = ROPE_THETA ** (-jnp.arange(0, half, 2, dtype=F32) / half)
    ar, ac = row[:, None] * inv, col[:, None] * inv
    z = jnp.zeros((seq, quarter), F32)
    cos = jnp.concatenate([jnp.cos(ar), jnp.cos(ar), jnp.cos(ac), jnp.cos(ac)], axis=1)
    sinm = jnp.concatenate([-jnp.sin(ar), z, -jnp.sin(ac), z], axis=1)
    sinp = jnp.concatenate([z, jnp.sin(ar), z, jnp.sin(ac)], axis=1)

    def place(a, fill):
        return jnp.full((seq, LANE), fill, F32).at[:, offset:offset + rot_dim].set(a)

    return jnp.stack([place(cos, 1.0), place(sinm, 0.0), place(sinp, 0.0)])


def _trunk(x, mod, wp, g_final):
    n_batch, seq, d = x.shape
    n_tok = n_batch * seq
    x2d = x.reshape(n_tok, d)
    tm = min(256, seq)
    ta = _rope_tables(seq, HEAD_DIM_A, 0)
    tb = _rope_tables(seq, ROPE_DIM_B, NOPE_DIM)

    qa, ka, vta, qb, kb, vtb, sig = _pre(x2d, mod, wp, ta, tb, seq, tm)
    ota = _attention(qa, ka, vta, n_batch=n_batch, seq=seq, n_groups=N_KV_A, n_sub=GROUP_A,
                     tq=min(ATTN_COLS // GROUP_A, seq), dv=HEAD_DIM_A)
    otb = _attention(qb, kb, vtb, n_batch=n_batch, seq=seq, n_groups=N_HEADS_B, n_sub=1,
                     tq=min(ATTN_COLS, seq), dv=V_DIM_B)
    x1, u2, eid, rank, gate, cnt = _post(x2d, mod, ota, otb, sig, wp, seq, tm)

    n_assign = n_tok * TOP_K
    n_blocks = -(-n_assign // EXPERT_BLOCK) + N_EXPERTS
    cap = n_blocks * EXPERT_BLOCK
    counts = cnt[0, :N_EXPERTS].astype(jnp.int32)
    padded = (counts + EXPERT_BLOCK - 1) // EXPERT_BLOCK * EXPERT_BLOCK
    pad_end = jnp.cumsum(padded)
    pad_start = pad_end - padded
    dest = pad_start[eid[:, :TOP_K]] + rank[:, :TOP_K]
    block_lo = jnp.arange(n_blocks, dtype=jnp.int32) * EXPERT_BLOCK
    block_e = jnp.minimum(jnp.sum(block_lo[:, None] >= pad_end[None, :], axis=1), N_EXPERTS - 1).astype(jnp.int32)
    block_valid = (block_lo < pad_end[-1]).astype(jnp.int32)

    xs = _dispatch(u2, dest, pad_end.astype(jnp.int32), cap, tm)
    yb = _experts(xs, block_e, block_valid, wp)
    out = _combine(x1, gate, mod, g_final.reshape(1, d), yb, dest, seq, min(128, seq))
    return out.reshape(n_batch, seq, d)


def kernel(x_prompt, x_sample, c_prompt, c_sample, w_mod, b_mod, g_attn, g_moe, w_in, g_qn, g_kn, g_qlat,
           w_uq, g_kvlat, w_ukv, w_br_a, w_br_b, w_out, w_router, b_router, w_gu, b_gu, w_down, b_down,
           g_final):
    assert w_mod.shape[0] == 1, "single-layer trunk"
    wp = _prep_weights(g_attn[0], g_moe[0], w_in[0], g_qn[0], g_kn[0], g_qlat[0], w_uq[0], g_kvlat[0],
                       w_ukv[0], w_br_a[0], w_br_b[0], w_out[0], w_router[0], b_router[0], w_gu[0],
                       b_gu[0], w_down[0], b_down[0])
    nb_p = c_prompt.shape[0]
    mod = _modulation(jnp.concatenate([c_prompt, c_sample], axis=0), w_mod[0], b_mod[0])
    y_prompt = _trunk(x_prompt, mod[:nb_p], wp, g_final)
    y_sample = _trunk(x_sample, mod[nb_p:], wp, g_final)
    return (y_prompt, y_sample)
```

```python
import functools

import jax
import jax.numpy as jnp
from jax import lax
from jax.experimental import pallas as pl
from jax.experimental.pallas import tpu as pltpu

F32 = jnp.float32
BF16 = jnp.bfloat16

D_MODEL = 1024
GRID_W = 64
ROPE_THETA = 10000.0
EPS = 1e-6
N_HEADS_A = 8
N_KV_A = 2
GROUP_A = N_HEADS_A // N_KV_A
HEAD_DIM_A = 64
N_HEADS_B = 8
Q_LORA = 768
KV_LORA = 256
NOPE_DIM = 64
ROPE_DIM_B = 32
V_DIM_B = 64
N_EXPERTS = 32
TOP_K = 4
D_FF = D_MODEL
SWIGLU_ALPHA = 1.702
SWIGLU_LIMIT = 7.0
EXPERT_BLOCK = 256

LANE = 128
BF16_SUBLANES = 16
NEG_BIG = -1e30
LOG2E = 1.4426950408889634
ONES_ROWS = BF16_SUBLANES
VMEM_LIMIT = 52 * 1024 * 1024
ATTN_FLAGS = None
ATTN_LAG = 2
ATTN_RING = 4
ATTN_COLS = 1024
ATTN_UNROLL = 16
ROW_LOOP_UNROLL = 4
BASE_LANE = LANE - 1
DEN_MIN, DEN_MAX = 1e-30, 1e30

_C_QA = 0
_C_KA = _C_QA + N_HEADS_A * LANE
_C_QLAT = _C_KA + N_KV_A * LANE
_C_KVLAT = _C_QLAT + Q_LORA
_C_KPE = _C_KVLAT + KV_LORA
_C_GATE = _C_KPE + LANE
_C_END = _C_GATE + 2 * D_MODEL

_SH_A, _SC_A, _GT_A, _SH_M, _SC_M, _GT_M = range(6)
MOD_ROWS = 8

_NT = (((1,), (1,)), ((), ()))
_TN = (((0,), (0,)), ((), ()))


def _cparams(*sem, flags=None):
    return pltpu.CompilerParams(dimension_semantics=sem, vmem_limit_bytes=VMEM_LIMIT, flags=flags)


def _rms(x, g):
    return x * lax.rsqrt(jnp.mean(x * x, axis=-1, keepdims=True) + EPS) * g


def _mod_kernel(c_ref, w_ref, b_ref, o_ref):
    c = c_ref[...]
    s = c * jax.nn.sigmoid(c)
    o_ref[...] = jnp.dot(s.astype(BF16), w_ref[...].astype(BF16), preferred_element_type=F32) + b_ref[...]


def _modulation(c, w_mod, b_mod):
    nb, d = c.shape
    rows = -(-nb // 8) * 8
    cp = jnp.zeros((rows, d), F32).at[:nb].set(c)
    n_out = w_mod.shape[1]
    tn = 512
    out = pl.pallas_call(
        _mod_kernel,
        grid=(n_out // tn,),
        in_specs=[pl.BlockSpec((rows, d), lambda j: (0, 0)),
                  pl.BlockSpec((d, tn), lambda j: (0, j)),
                  pl.BlockSpec((1, tn), lambda j: (0, j))],
        out_specs=pl.BlockSpec((rows, tn), lambda j: (0, j)),
        out_shape=jax.ShapeDtypeStruct((rows, n_out), F32),
        compiler_params=_cparams("arbitrary"),
        name="mod",
    )(cp, w_mod, b_mod.reshape(1, n_out))
    mod = out[:nb].reshape(nb, 6, d)
    return jnp.concatenate([mod, jnp.zeros((nb, MOD_ROWS - 6, d), F32)], axis=1)


def _rope(x, cos, sinm, sinp, shift):
    return x * cos + pltpu.roll(x, LANE - shift, 1) * sinm + pltpu.roll(x, shift, 1) * sinp


def _pre_kernel(x_ref, mod_ref, gattn_ref, w1_ref, wvat_ref, gqa_ref, gka_ref, gqlat_ref, wuq_ref,
                gkvlat_ref, wuk_ref, wuvt_ref, ta_ref, tb_ref,
                qa_ref, ka_ref, vta_ref, qb_ref, kb_ref, vtb_ref, sig_ref):
    x = x_ref[...]
    u = _rms(x, gattn_ref[...]) * (1.0 + mod_ref[_SC_A:_SC_A + 1, :]) + mod_ref[_SH_A:_SH_A + 1, :]
    ub = u.astype(BF16)

    def proj(lo, hi):
        return jnp.dot(ub, w1_ref[:, lo:hi], preferred_element_type=F32)

    cos_a, sinm_a, sinp_a = ta_ref[0], ta_ref[1], ta_ref[2]
    cos_b, sinm_b, sinp_b = tb_ref[0], tb_ref[1], tb_ref[2]
    shift_a = HEAD_DIM_A // 4
    shift_b = ROPE_DIM_B // 4

    def head_a(xs, g, scale):
        ms = jnp.sum(xs * xs, axis=-1, keepdims=True) * (1.0 / HEAD_DIM_A)
        xn = xs * lax.rsqrt(ms + EPS) * g
        return (_rope(xn, cos_a, sinm_a, sinp_a, shift_a) * scale).astype(BF16)

    qa = proj(_C_QA, _C_KA)
    for h in range(N_HEADS_A):
        qa_ref[:, h * LANE:(h + 1) * LANE] = head_a(qa[:, h * LANE:(h + 1) * LANE], gqa_ref[...],
                                                    HEAD_DIM_A ** -0.5 * LOG2E)
    ka = proj(_C_KA, _C_QLAT)
    base_lane = lax.broadcasted_iota(jnp.int32, (x.shape[0], LANE), 1) == BASE_LANE
    for h in range(N_KV_A):
        kh = head_a(ka[:, h * LANE:(h + 1) * LANE], gka_ref[...], 1.0)
        ka_ref[:, h * LANE:(h + 1) * LANE] = jnp.where(base_lane, 1.0, kh).astype(BF16)
    vta = lax.dot_general(wvat_ref[...], ub, _NT, preferred_element_type=F32)
    vta_ref[0, :, :HEAD_DIM_A, :] = vta.astype(BF16).reshape(N_KV_A, HEAD_DIM_A, vta.shape[-1])
    vta_ref[0, :, HEAD_DIM_A:, :] = jnp.ones((N_KV_A, ONES_ROWS, vta.shape[-1]), BF16)

    qn = _rms(proj(_C_QLAT, _C_KVLAT), gqlat_ref[...]).astype(BF16)
    qb = jnp.dot(qn, wuq_ref[...], preferred_element_type=F32)
    scale_b = (NOPE_DIM + ROPE_DIM_B) ** -0.5 * LOG2E
    for h in range(N_HEADS_B):
        sl = slice(h * LANE, (h + 1) * LANE)
        qb_ref[:, sl] = (_rope(qb[:, sl], cos_b, sinm_b, sinp_b, shift_b) * scale_b).astype(BF16)
    kvn = _rms(proj(_C_KVLAT, _C_KPE), gkvlat_ref[...]).astype(BF16)
    kpe = _rope(proj(_C_KPE, _C_GATE), cos_b, sinm_b, sinp_b, shift_b)
    kb = jnp.dot(kvn, wuk_ref[...], preferred_element_type=F32)
    for h in range(N_HEADS_B):
        sl = slice(h * LANE, (h + 1) * LANE)
        kb_ref[:, sl] = jnp.where(base_lane, 1.0, kb[:, sl] + kpe).astype(BF16)
    vtb = lax.dot_general(wuvt_ref[...], kvn, _NT, preferred_element_type=F32)
    vtb_ref[0, :, :V_DIM_B, :] = vtb.astype(BF16).reshape(N_HEADS_B, V_DIM_B, vtb.shape[-1])
    vtb_ref[0, :, V_DIM_B:, :] = jnp.ones((N_HEADS_B, ONES_ROWS, vtb.shape[-1]), BF16)

    sig_ref[...] = jax.nn.sigmoid(proj(_C_GATE, _C_END)).astype(BF16)


def _pre(x2d, mod, wp, tables_a, tables_b, seq, tm):
    n_tok, d = x2d.shape
    n_tiles = n_tok // tm
    tiles_per_seq = seq // tm
    full = lambda a: pl.BlockSpec(a.shape, lambda i: (0,) * a.ndim)
    tok = lambda cols: pl.BlockSpec((tm, cols), lambda i: (i, 0))
    tab = pl.BlockSpec((3, tm, LANE), lambda i: (0, i % tiles_per_seq, 0))
    in_specs = [tok(d),
                pl.BlockSpec((None, MOD_ROWS, d), lambda i: (i // tiles_per_seq, 0, 0)),
                full(wp["g_attn"]), full(wp["w1"]), full(wp["wvat"]), full(wp["g_qa"]), full(wp["g_ka"]),
                full(wp["g_qlat"]), full(wp["wuq"]), full(wp["g_kvlat"]), full(wp["wuk"]), full(wp["wuvt"]),
                tab, tab]
    out_shape = [
        jax.ShapeDtypeStruct((n_tok, N_HEADS_A * LANE), BF16),
        jax.ShapeDtypeStruct((n_tok, N_KV_A * LANE), BF16),
        jax.ShapeDtypeStruct((n_tiles, N_KV_A, HEAD_DIM_A + ONES_ROWS, tm), BF16),
        jax.ShapeDtypeStruct((n_tok, N_HEADS_B * LANE), BF16),
        jax.ShapeDtypeStruct((n_tok, N_HEADS_B * LANE), BF16),
        jax.ShapeDtypeStruct((n_tiles, N_HEADS_B, V_DIM_B + ONES_ROWS, tm), BF16),
        jax.ShapeDtypeStruct((n_tok, 2 * D_MODEL), BF16),
    ]
    out_specs = [tok(N_HEADS_A * LANE), tok(N_KV_A * LANE),
                 pl.BlockSpec((1, N_KV_A, HEAD_DIM_A + ONES_ROWS, tm), lambda i: (i, 0, 0, 0)),
                 tok(N_HEADS_B * LANE), tok(N_HEADS_B * LANE),
                 pl.BlockSpec((1, N_HEADS_B, V_DIM_B + ONES_ROWS, tm), lambda i: (i, 0, 0, 0)),
                 tok(2 * D_MODEL)]
    return pl.pallas_call(
        _pre_kernel, grid=(n_tiles,), in_specs=in_specs, out_specs=out_specs, out_shape=out_shape,
        compiler_params=_cparams("arbitrary"), name="pre",
    )(x2d, mod, wp["g_attn"], wp["w1"], wp["wvat"], wp["g_qa"], wp["g_ka"], wp["g_qlat"], wp["wuq"],
      wp["g_kvlat"], wp["wuk"], wp["wuvt"], tables_a, tables_b)


def _attn_kernel(q_ref, k_ref, vt_ref, o_ref, kmax_ref, *p_bufs, n_sub, tq, tk, n_chunks, dv, unroll):
    lag = ATTN_LAG
    cols = n_sub * tq
    if n_sub == 1:
        q = q_ref[...]
    else:
        q = jnp.concatenate([q_ref[:, j * LANE:(j + 1) * LANE] for j in range(n_sub)], axis=0)

    def key_chunk(i):
        return k_ref[pl.ds(pl.multiple_of(i * tk, tk), tk), :]

    @pl.when(pl.program_id(2) == 0)
    def _():
        def norm_body(c, mx):
            kc = key_chunk(c).astype(F32)
            n2 = jnp.sum(kc * kc, axis=-1, keepdims=True)
            return jnp.maximum(mx, jnp.max(n2, axis=0, keepdims=True))

        kmax2 = lax.fori_loop(0, n_chunks, norm_body, jnp.zeros((1, 1), F32))
        kmax_ref[...] = jnp.broadcast_to(jnp.sqrt(jnp.maximum(kmax2 - 1.0, 0.0)), kmax_ref.shape)

    qt = q.astype(F32).T
    base = jnp.sqrt(jnp.sum(qt * qt, axis=0, keepdims=True)) * kmax_ref[0:1, 0:1]
    row = lax.broadcasted_iota(jnp.int32, qt.shape, 0)
    qt_aug = jnp.where(row == BASE_LANE, -base, qt).astype(BF16)

    def pv(c, p_ref):
        return jnp.dot(vt_ref[c], p_ref[...], preferred_element_type=F32)

    def trip(i, u, acc):
        acc = acc + pv(jnp.maximum(i - lag, 0), p_bufs[(u - lag) % ATTN_RING])
        s = jnp.dot(key_chunk(i), qt_aug, preferred_element_type=F32)
        p_bufs[u % ATTN_RING][...] = jnp.exp2(s).astype(BF16)
        return acc

    def body(j, acc):
        for u in range(unroll):
            acc = trip(j * unroll + u, u, acc)
        return acc

    for c in range(lag):
        p_bufs[(-1 - c) % ATTN_RING][...] = jnp.zeros((tk, cols), BF16)
    acc = lax.fori_loop(0, n_chunks // unroll, body, jnp.zeros((dv + ONES_ROWS, cols), F32))
    for c in range(n_chunks - lag, n_chunks):
        acc = acc + pv(c, p_bufs[c % ATTN_RING])

    def emit(acc):
        o = acc[:dv] / acc[dv:dv + 1]
        for j in range(n_sub):
            o_ref[j * dv:(j + 1) * dv, :] = o[:, j * tq:(j + 1) * tq].astype(BF16)

    emit(acc)
    den = acc[dv:dv + 1]
    safe = jnp.logical_and(den > DEN_MIN, den < DEN_MAX)
    n_unsafe = jnp.sum(jnp.where(safe, 0.0, 1.0))

    @pl.when(n_unsafe > 0.0)
    def _():
        def exact_body(i, carry):
            m, acc = carry
            s = lax.dot_general(key_chunk(i), q, _NT, preferred_element_type=F32)
            m_new = jnp.maximum(m, jnp.max(s, axis=0, keepdims=True))
            p = jnp.exp2(s - m_new).astype(BF16)
            return m_new, jnp.exp2(m - m_new) * acc + jnp.dot(vt_ref[i], p, preferred_element_type=F32)

        init = (jnp.full((1, cols), NEG_BIG, F32), jnp.zeros((dv + ONES_ROWS, cols), F32))
        emit(lax.fori_loop(0, n_chunks, exact_body, init)[1])


def _attention(q, k, vt, *, n_batch, seq, n_groups, n_sub, tq, dv):
    n_tok = q.shape[0]
    tk = vt.shape[-1]
    n_chunks = seq // tk
    q_tiles = seq // tq
    unroll = ATTN_UNROLL
    while n_chunks % unroll:
        unroll //= 2
    assert unroll % ATTN_RING == 0, "buffer slots must be static per trip"
    kern = functools.partial(_attn_kernel, n_sub=n_sub, tq=tq, tk=tk, n_chunks=n_chunks, dv=dv,
                             unroll=unroll)
    cols = n_sub * tq
    return pl.pallas_call(
        kern,
        grid=(n_batch, n_groups, q_tiles),
        in_specs=[pl.BlockSpec((tq, n_sub * LANE), lambda b, g, i: (b * q_tiles + i, g)),
                  pl.BlockSpec((seq, LANE), lambda b, g, i: (b, g)),
                  pl.BlockSpec((n_chunks, None, dv + ONES_ROWS, tk), lambda b, g, i: (b, g, 0, 0))],
        out_specs=pl.BlockSpec((n_sub * dv, tq), lambda b, g, i: (g, b * q_tiles + i)),
        out_shape=jax.ShapeDtypeStruct((n_groups * n_sub * dv, n_tok), BF16),
        scratch_shapes=[pltpu.VMEM((8, LANE), F32)] + [pltpu.VMEM((tk, cols), BF16)] * ATTN_RING,
        compiler_params=_cparams("arbitrary", "arbitrary", "arbitrary", flags=ATTN_FLAGS),
        name="attn",
    )(q, k, vt)


def _post_kernel(x_ref, mod_ref, ota_ref, otb_ref, sig_ref, wbra_ref, wbrb_ref, wout_ref, gmoe_ref,
                 wr_ref, br_ref,
                 x1_ref, u2_ref, eid_ref, rank_ref, gate_ref, cnt_ref, carry_ref):
    i = pl.program_id(0)

    @pl.when(i == 0)
    def _():
        carry_ref[...] = jnp.zeros_like(carry_ref)

    d = D_MODEL
    ya = lax.dot_general(ota_ref[...], wbra_ref[...], _TN, preferred_element_type=F32)
    yb = lax.dot_general(otb_ref[...], wbrb_ref[...], _TN, preferred_element_type=F32)
    mixed = sig_ref[:, :d].astype(F32) * ya + sig_ref[:, d:].astype(F32) * yb
    att = jnp.dot(mixed.astype(BF16), wout_ref[...], preferred_element_type=F32)
    x1 = x_ref[...] + mod_ref[_GT_A:_GT_A + 1, :] * att
    x1_ref[...] = x1
    u2 = _rms(x1, gmoe_ref[...]) * (1.0 + mod_ref[_SC_M:_SC_M + 1, :]) + mod_ref[_SH_M:_SH_M + 1, :]
    u2_ref[...] = u2.reshape(u2_ref.shape)

    logits = jnp.dot(u2.astype(BF16), wr_ref[...], preferred_element_type=F32) + br_ref[...]
    tm = logits.shape[0]
    lane = lax.broadcasted_iota(jnp.int32, (tm, LANE), 1)
    vals = logits
    top_v, top_i = [], []
    for _ in range(TOP_K):
        mx = jnp.max(vals, axis=-1, keepdims=True)
        idx = jnp.min(jnp.where(vals == mx, lane, LANE), axis=-1, keepdims=True)
        top_v.append(mx)
        top_i.append(idx)
        vals = jnp.where(lane == idx, -jnp.inf, vals)
    ex = [jnp.exp(v - top_v[0]) for v in top_v]
    den = ex[0] + ex[1] + ex[2] + ex[3]

    onehot = [lane == idx for idx in top_i]
    cnt = sum(oh.astype(F32) for oh in onehot)
    r_i = lax.broadcasted_iota(jnp.int32, (tm, tm), 0)
    c_i = lax.broadcasted_iota(jnp.int32, (tm, tm), 1)
    lower = (c_i < r_i).astype(BF16)
    before = jnp.dot(lower, cnt.astype(BF16), preferred_element_type=F32) + carry_ref[...]
    eid = jnp.zeros((tm, LANE), jnp.int32)
    rank = jnp.zeros((tm, LANE), jnp.int32)
    gate = jnp.zeros((tm, LANE), F32)
    for k in range(TOP_K):
        rk = jnp.sum(jnp.where(onehot[k], before, 0.0), axis=-1, keepdims=True).astype(jnp.int32)
        eid = jnp.where(lane == k, top_i[k], eid)
        rank = jnp.where(lane == k, rk, rank)
        gate = jnp.where(lane == k, ex[k] / den, gate)
    eid_ref[...] = eid
    rank_ref[...] = rank
    gate_ref[...] = gate
    carry = carry_ref[...] + jnp.sum(cnt, axis=0, keepdims=True)
    carry_ref[...] = carry
    cnt_ref[...] = carry


def _post(x2d, mod, ota, otb, sig, wp, seq, tm):
    n_tok, d = x2d.shape
    n_tiles = n_tok // tm
    tiles_per_seq = seq // tm
    full = lambda a: pl.BlockSpec(a.shape, lambda i: (0,) * a.ndim)
    tok = lambda cols: pl.BlockSpec((tm, cols), lambda i: (i, 0))
    tcol = lambda rows: pl.BlockSpec((rows, tm), lambda i: (0, i))
    in_specs = [tok(d), pl.BlockSpec((None, MOD_ROWS, d), lambda i: (i // tiles_per_seq, 0, 0)),
                tcol(ota.shape[0]), tcol(otb.shape[0]), tok(2 * d),
                full(wp["wbra"]), full(wp["wbrb"]), full(wp["wout"]), full(wp["g_moe"]),
                full(wp["wr"]), full(wp["br"])]
    out_shape = [jax.ShapeDtypeStruct((n_tok, d), F32), jax.ShapeDtypeStruct((n_tok, d // LANE, LANE), F32),
                 jax.ShapeDtypeStruct((n_tok, LANE), jnp.int32), jax.ShapeDtypeStruct((n_tok, LANE), jnp.int32),
                 jax.ShapeDtypeStruct((n_tok, LANE), F32), jax.ShapeDtypeStruct((1, LANE), F32)]
    out_specs = [tok(d), pl.BlockSpec((tm, d // LANE, LANE), lambda i: (i, 0, 0)),
                 tok(LANE), tok(LANE), tok(LANE), pl.BlockSpec((1, LANE), lambda i: (0, 0))]
    return pl.pallas_call(
        _post_kernel, grid=(n_tiles,), in_specs=in_specs, out_specs=out_specs, out_shape=out_shape,
        scratch_shapes=[pltpu.VMEM((1, LANE), F32)],
        compiler_params=_cparams("arbitrary"), name="post",
    )(x2d, mod, ota, otb, sig, wp["wbra"], wp["wbrb"], wp["wout"], wp["g_moe"], wp["wr"], wp["br"])


def _row_copy(src, dst, sem):
    return pltpu.make_async_copy(src, dst, sem)


def _dispatch_kernel(pad_end_ref, dest_ref, u_ref, xs_ref, zero_ref, sem, zero_sem, *, tm):
    @pl.when(pl.program_id(0) == 0)
    def _():
        zero_ref[...] = jnp.zeros_like(zero_ref)

        def tail_copy(e):
            start = pl.multiple_of(pad_end_ref[e] - EXPERT_BLOCK, EXPERT_BLOCK)
            return pltpu.make_async_copy(zero_ref, xs_ref.at[pl.ds(start, EXPERT_BLOCK)], zero_sem)

        def nonempty(e):
            return pad_end_ref[e] > (pad_end_ref[e - 1] if e else 0)

        for e in range(N_EXPERTS):
            pl.when(nonempty(e))(lambda e=e: tail_copy(e).start())
        for e in range(N_EXPERTS):
            pl.when(nonempty(e))(lambda e=e: tail_copy(e).wait())

        def unused_copy(b):
            start = pl.multiple_of(b * EXPERT_BLOCK, EXPERT_BLOCK)
            return pltpu.make_async_copy(zero_ref, xs_ref.at[pl.ds(start, EXPERT_BLOCK)], zero_sem)

        first_unused = pad_end_ref[N_EXPERTS - 1] // EXPERT_BLOCK
        n_blocks = xs_ref.shape[0] // EXPERT_BLOCK
        lax.fori_loop(first_unused, n_blocks, lambda b, c: (unused_copy(b).start(), c)[1], 0)
        lax.fori_loop(first_unused, n_blocks, lambda b, c: (unused_copy(b).wait(), c)[1], 0)

    def issue(r, c):
        for k in range(TOP_K):
            d = dest_ref[0, 0, r * TOP_K + k]
            _row_copy(u_ref.at[pl.ds(r, 1)], xs_ref.at[pl.ds(d, 1)], sem).start(priority=k % 2)
        return c

    lax.fori_loop(0, tm, issue, 0, unroll=ROW_LOOP_UNROLL)

    def drain(r, c):
        for k in range(TOP_K):
            _row_copy(u_ref.at[pl.ds(0, 1)], xs_ref.at[pl.ds(0, 1)], sem).wait()
        return c

    lax.fori_loop(0, tm, drain, 0, unroll=ROW_LOOP_UNROLL)


def _dispatch(u2, dest, pad_end, cap, tm):
    n_tok, row = u2.shape[0], u2.shape[1:]
    n_tiles = n_tok // tm
    dest3 = dest.reshape(n_tiles, 1, tm * TOP_K)
    grid_spec = pltpu.PrefetchScalarGridSpec(
        num_scalar_prefetch=1,
        grid=(n_tiles,),
        in_specs=[pl.BlockSpec((1, 1, tm * TOP_K), lambda i, pe: (i, 0, 0), memory_space=pltpu.SMEM),
                  pl.BlockSpec((tm,) + row, lambda i, pe: (i, 0, 0))],
        out_specs=pl.BlockSpec(memory_space=pl.ANY),
        scratch_shapes=[pltpu.VMEM((EXPERT_BLOCK,) + row, u2.dtype), pltpu.SemaphoreType.DMA(()),
                        pltpu.SemaphoreType.DMA(())],
    )
    return pl.pallas_call(
        functools.partial(_dispatch_kernel, tm=tm), grid_spec=grid_spec,
        out_shape=jax.ShapeDtypeStruct((cap,) + row, u2.dtype),
        compiler_params=_cparams("arbitrary"), name="dispatch",
    )(pad_end, dest3, u2)


def _expert_kernel(be_ref, valid_ref, xs_ref, wgu_ref, bgu_ref, wd_ref, bd_ref, y_ref, wgu_bf, wd_bf):
    b = pl.program_id(0)

    @pl.when(jnp.logical_or(b == 0, be_ref[b] != be_ref[jnp.maximum(b - 1, 0)]))
    def _():
        wgu_bf[...] = wgu_ref[0].astype(BF16)
        wd_bf[...] = wd_ref[0].astype(BF16)

    @pl.when(valid_ref[b] == 1)
    def _():
        x = xs_ref[...].reshape(EXPERT_BLOCK, D_MODEL).astype(BF16)
        h = jnp.dot(x, wgu_bf[...], preferred_element_type=F32) + bgu_ref[0]
        hg = jnp.minimum(h[:, :D_FF], SWIGLU_LIMIT)
        hl = jnp.clip(h[:, D_FF:], -SWIGLU_LIMIT, SWIGLU_LIMIT)
        a = hg * jax.nn.sigmoid(SWIGLU_ALPHA * hg) * (hl + 1.0)
        y = jnp.dot(a.astype(BF16), wd_bf[...], preferred_element_type=F32) + bd_ref[0]
        y_ref[...] = y.reshape(y_ref.shape)

    @pl.when(valid_ref[b] == 0)
    def _():
        y_ref[...] = jnp.zeros_like(y_ref)


def _experts(xs, block_e, block_valid, wp):
    cap, d = xs.shape[0], D_MODEL
    n_blocks = cap // EXPERT_BLOCK
    grid_spec = pltpu.PrefetchScalarGridSpec(
        num_scalar_prefetch=2,
        grid=(n_blocks,),
        in_specs=[pl.BlockSpec((EXPERT_BLOCK,) + xs.shape[1:], lambda b, be, bv: (jnp.where(bv[b] == 1, b, 0), 0, 0)),
                  pl.BlockSpec((1, d, 2 * D_FF), lambda b, be, bv: (be[b], 0, 0)),
                  pl.BlockSpec((1, 1, 2 * D_FF), lambda b, be, bv: (be[b], 0, 0)),
                  pl.BlockSpec((1, D_FF, d), lambda b, be, bv: (be[b], 0, 0)),
                  pl.BlockSpec((1, 1, d), lambda b, be, bv: (be[b], 0, 0))],
        out_specs=pl.BlockSpec((EXPERT_BLOCK,) + xs.shape[1:], lambda b, be, bv: (b, 0, 0)),
        scratch_shapes=[pltpu.VMEM((d, 2 * D_FF), BF16), pltpu.VMEM((D_FF, d), BF16)],
    )
    return pl.pallas_call(
        _expert_kernel, grid_spec=grid_spec,
        out_shape=jax.ShapeDtypeStruct(xs.shape, F32),
        compiler_params=_cparams("arbitrary"), name="experts",
    )(block_e, block_valid, xs, wp["wgu"], wp["bgu"], wp["wd"], wp["bd"])


def _combine_kernel(dest_ref, dest_next_ref, x1_ref, gate_ref, mod_ref, gfin_ref, y_hbm, o_ref, buf, sem, *, tm):
    i = pl.program_id(0)
    slot = i % 2

    def gather(dref, s):
        def issue(r, c):
            for k in range(TOP_K):
                d = dref[0, 0, r * TOP_K + k]
                _row_copy(y_hbm.at[pl.ds(d, 1)], buf.at[s, k, pl.ds(r, 1)], sem.at[s]).start(priority=k % 2)
            return c

        lax.fori_loop(0, tm, issue, 0, unroll=ROW_LOOP_UNROLL)

    @pl.when(i == 0)
    def _():
        gather(dest_ref, slot)

    @pl.when(i + 1 < pl.num_programs(0))
    def _():
        gather(dest_next_ref, 1 - slot)

    def drain(r, c):
        for k in range(TOP_K):
            _row_copy(y_hbm.at[pl.ds(0, 1)], buf.at[slot, k, pl.ds(0, 1)], sem.at[slot]).wait()
        return c

    lax.fori_loop(0, tm, drain, 0, unroll=ROW_LOOP_UNROLL)
    gate = gate_ref[...]
    rows = lambda k: buf[slot, k].reshape(tm, D_MODEL)
    y = gate[:, 0:1] * rows(0)
    for k in range(1, TOP_K):
        y = y + gate[:, k:k + 1] * rows(k)
    x2 = x1_ref[...] + mod_ref[_GT_M:_GT_M + 1, :] * y
    o_ref[...] = _rms(x2, gfin_ref[...])


def _combine(x1, gate, mod, g_final, yb, dest, seq, tm):
    n_tok, d = x1.shape
    n_tiles = n_tok // tm
    tiles_per_seq = seq // tm
    dest3 = dest.reshape(n_tiles, 1, tm * TOP_K)
    return pl.pallas_call(
        functools.partial(_combine_kernel, tm=tm),
        grid=(n_tiles,),
        in_specs=[pl.BlockSpec((1, 1, tm * TOP_K), lambda i: (i, 0, 0), memory_space=pltpu.SMEM),
                  pl.BlockSpec((1, 1, tm * TOP_K), lambda i: (jnp.minimum(i + 1, n_tiles - 1), 0, 0),
                               memory_space=pltpu.SMEM),
                  pl.BlockSpec((tm, d), lambda i: (i, 0)),
                  pl.BlockSpec((tm, LANE), lambda i: (i, 0)),
                  pl.BlockSpec((None, MOD_ROWS, d), lambda i: (i // tiles_per_seq, 0, 0)),
                  pl.BlockSpec((1, d), lambda i: (0, 0)),
                  pl.BlockSpec(memory_space=pl.ANY)],
        out_specs=pl.BlockSpec((tm, d), lambda i: (i, 0)),
        out_shape=jax.ShapeDtypeStruct((n_tok, d), F32),
        scratch_shapes=[pltpu.VMEM((2, TOP_K, tm) + yb.shape[1:], F32), pltpu.SemaphoreType.DMA((2,))],
        compiler_params=_cparams("arbitrary"), name="combine",
    )(dest3, dest3, x1, gate, mod, g_final, yb)


def _slot_cols(w, n_heads, width, offset=0):
    k = w.shape[0]
    w3 = w.reshape(k, n_heads, width)
    out = jnp.zeros((k, n_heads, LANE), w.dtype).at[:, :, offset:offset + width].set(w3)
    return out.reshape(k, n_heads * LANE)


def _prep_weights(g_attn, g_moe, w_in, g_qn, g_kn, g_qlat, w_uq, g_kvlat, w_ukv, w_br_a, w_br_b, w_out,
                  w_router, b_router, w_gu, b_gu, w_down, b_down):
    d = D_MODEL
    wa = N_HEADS_A * HEAD_DIM_A
    wk = N_KV_A * HEAD_DIM_A
    c0 = 0
    w_qa = w_in[:, c0:c0 + wa]; c0 += wa
    w_ka = w_in[:, c0:c0 + wk]; c0 += wk
    w_va = w_in[:, c0:c0 + wk]; c0 += wk
    w_ql = w_in[:, c0:c0 + Q_LORA]; c0 += Q_LORA
    w_kvl = w_in[:, c0:c0 + KV_LORA]; c0 += KV_LORA
    w_kpe = w_in[:, c0:c0 + ROPE_DIM_B]; c0 += ROPE_DIM_B
    w_gate = w_in[:, c0:]
    w1 = jnp.concatenate([
        _slot_cols(w_qa, N_HEADS_A, HEAD_DIM_A), _slot_cols(w_ka, N_KV_A, HEAD_DIM_A), w_ql, w_kvl,
        _slot_cols(w_kpe, 1, ROPE_DIM_B, NOPE_DIM), w_gate], axis=1).astype(BF16)
    ukv = w_ukv.reshape(KV_LORA, N_HEADS_B, NOPE_DIM + V_DIM_B)
    w_uk = ukv[:, :, :NOPE_DIM].reshape(KV_LORA, N_HEADS_B * NOPE_DIM)
    w_uv = ukv[:, :, NOPE_DIM:].reshape(KV_LORA, N_HEADS_B * V_DIM_B)
    pad_gain = lambda g: jnp.zeros((1, LANE), F32).at[0, :g.shape[0]].set(g)
    return {
        "g_attn": g_attn.reshape(1, d), "g_moe": g_moe.reshape(1, d),
        "w1": w1, "wvat": w_va.T.astype(BF16),
        "g_qa": pad_gain(g_qn), "g_ka": pad_gain(g_kn),
        "g_qlat": g_qlat.reshape(1, Q_LORA), "g_kvlat": g_kvlat.reshape(1, KV_LORA),
        "wuq": _slot_cols(w_uq, N_HEADS_B, NOPE_DIM + ROPE_DIM_B).astype(BF16),
        "wuk": _slot_cols(w_uk, N_HEADS_B, NOPE_DIM).astype(BF16),
        "wuvt": w_uv.T.astype(BF16),
        "wbra": w_br_a.astype(BF16), "wbrb": w_br_b.astype(BF16), "wout": w_out.astype(BF16),
        "wr": jnp.zeros((d, LANE), F32).at[:, :N_EXPERTS].set(w_router).astype(BF16),
        "br": jnp.full((1, LANE), NEG_BIG, F32).at[0, :N_EXPERTS].set(b_router),
        "wgu": w_gu, "bgu": b_gu.reshape(N_EXPERTS, 1, 2 * D_FF),
        "wd": w_down, "bd": b_down.reshape(N_EXPERTS, 1, d),
    }


def _rope_tables(seq, rot_dim, offset):
    half = rot_dim // 2
    quarter = half // 2
    t = jnp.arange(seq, dtype=jnp.int32)
    row = (t // GRID_W).astype(F32)
    col = (t % GRID_W).astype(F32)
    inv = ROPE_THETA ** (-jnp.arange(0, half, 2, dtype=F32) / half)
    ar, ac = row[:, None] * inv, col[:, None] * inv
    z = jnp.zeros((seq, quarter), F32)
    cos = jnp.concatenate([jnp.cos(ar), jnp.cos(ar), jnp.cos(ac), jnp.cos(ac)], axis=1)
    sinm = jnp.concatenate([-jnp.sin(ar), z, -jnp.sin(ac), z], axis=1)
    sinp = jnp.concatenate([z, jnp.sin(ar), z, jnp.sin(ac)], axis=1)

    def place(a, fill):
        return jnp.full((seq, LANE), fill, F32).at[:, offset:offset + rot_dim].set(a)

    return jnp.stack([place(cos, 1.0), place(sinm, 0.0), place(sinp, 0.0)])


def _trunk(x, mod, wp, g_final):
    n_batch, seq, d = x.shape
    n_tok = n_batch * seq
    x2d = x.reshape(n_tok, d)
    tm = min(256, seq)
    ta = _rope_tables(seq, HEAD_DIM_A, 0)
    tb = _rope_tables(seq, ROPE_DIM_B, NOPE_DIM)

    qa, ka, vta, qb, kb, vtb, sig = _pre(x2d, mod, wp, ta, tb, seq, tm)
    ota = _attention(qa, ka, vta, n_batch=n_batch, seq=seq, n_groups=N_KV_A, n_sub=GROUP_A,
                     tq=min(ATTN_COLS // GROUP_A, seq), dv=HEAD_DIM_A)
    otb = _attention(qb, kb, vtb, n_batch=n_batch, seq=seq, n_groups=N_HEADS_B, n_sub=1,
                     tq=min(ATTN_COLS, seq), dv=V_DIM_B)
    x1, u2, eid, rank, gate, cnt = _post(x2d, mod, ota, otb, sig, wp, seq, tm)

    n_assign = n_tok * TOP_K
    n_blocks = -(-n_assign // EXPERT_BLOCK) + N_EXPERTS
    cap = n_blocks * EXPERT_BLOCK
    counts = cnt[0, :N_EXPERTS].astype(jnp.int32)
    padded = (counts + EXPERT_BLOCK - 1) // EXPERT_BLOCK * EXPERT_BLOCK
    pad_end = jnp.cumsum(padded)
    pad_start = pad_end - padded
    dest = pad_start[eid[:, :TOP_K]] + rank[:, :TOP_K]
    block_lo = jnp.arange(n_blocks, dtype=jnp.int32) * EXPERT_BLOCK
    block_e = jnp.minimum(jnp.sum(block_lo[:, None] >= pad_end[None, :], axis=1), N_EXPERTS - 1).astype(jnp.int32)
    block_valid = (block_lo < pad_end[-1]).astype(jnp.int32)

    xs = _dispatch(u2, dest, pad_end.astype(jnp.int32), cap, tm)
    yb = _experts(xs, block_e, block_valid, wp)
    out = _combine(x1, gate, mod, g_final.reshape(1, d), yb, dest, seq, tm)
    return out.reshape(n_batch, seq, d)


def kernel(x_prompt, x_sample, c_prompt, c_sample, w_mod, b_mod, g_attn, g_moe, w_in, g_qn, g_kn, g_qlat,
           w_uq, g_kvlat, w_ukv, w_br_a, w_br_b, w_out, w_router, b_router, w_gu, b_gu, w_down, b_down,
           g_final):
    assert w_mod.shape[0] == 1, "single-layer trunk"
    wp = _prep_weights(g_attn[0], g_moe[0], w_in[0], g_qn[0], g_kn[0], g_qlat[0], w_uq[0], g_kvlat[0],
                       w_ukv[0], w_br_a[0], w_br_b[0], w_out[0], w_router[0], b_router[0], w_gu[0],
                       b_gu[0], w_down[0], b_down[0])
    nb_p = c_prompt.shape[0]
    mod = _modulation(jnp.concatenate([c_prompt, c_sample], axis=0), w_mod[0], b_mod[0])
    y_prompt = _trunk(x_prompt, mod[:nb_p], wp, g_final)
    y_sample = _trunk(x_sample, mod[nb_p:], wp, g_final)
    return (y_prompt, y_sample)
```

```python
import functools

import jax
import jax.numpy as jnp
from jax import lax
from jax.experimental import pallas as pl
from jax.experimental.pallas import tpu as pltpu

F32 = jnp.float32
BF16 = jnp.bfloat16

D_MODEL = 1024
GRID_W = 64
ROPE_THETA = 10000.0
EPS = 1e-6
N_HEADS_A = 8
N_KV_A = 2
GROUP_A = N_HEADS_A // N_KV_A
HEAD_DIM_A = 64
N_HEADS_B = 8
Q_LORA = 768
KV_LORA = 256
NOPE_DIM = 64
ROPE_DIM_B = 32
V_DIM_B = 64
N_EXPERTS = 32
TOP_K = 4
D_FF = D_MODEL
SWIGLU_ALPHA = 1.702
SWIGLU_LIMIT = 7.0
EXPERT_BLOCK = 256

LANE = 128
BF16_SUBLANES = 16
NEG_BIG = -1e30
LOG2E = 1.4426950408889634
ONES_ROWS = BF16_SUBLANES
VMEM_LIMIT = 52 * 1024 * 1024
ATTN_LAG = 2
ATTN_RING = 4
ATTN_COLS = 1024
ATTN_UNROLL = 32
DISPATCH_TILE = 512
ROW_LOOP_UNROLL = 4
BASE_LANE = LANE - 1
DEN_MIN, DEN_MAX = 1e-30, 1e30

_C_QA = 0
_C_KA = _C_QA + N_HEADS_A * LANE
_C_QLAT = _C_KA + N_KV_A * LANE
_C_KVLAT = _C_QLAT + Q_LORA
_C_KPE = _C_KVLAT + KV_LORA
_C_GATE = _C_KPE + LANE
_C_END = _C_GATE + 2 * D_MODEL

_SH_A, _SC_A, _GT_A, _SH_M, _SC_M, _GT_M = range(6)
MOD_ROWS = 8

_NT = (((1,), (1,)), ((), ()))
_TN = (((0,), (0,)), ((), ()))


def _cparams(*sem):
    return pltpu.CompilerParams(dimension_semantics=sem, vmem_limit_bytes=VMEM_LIMIT)


def _rms(x, g):
    return x * lax.rsqrt(jnp.mean(x * x, axis=-1, keepdims=True) + EPS) * g


def _mod_kernel(c_ref, w_ref, b_ref, o_ref):
    c = c_ref[...]
    s = c * jax.nn.sigmoid(c)
    o_ref[...] = jnp.dot(s.astype(BF16), w_ref[...].astype(BF16), preferred_element_type=F32) + b_ref[...]


def _modulation(c, w_mod, b_mod):
    nb, d = c.shape
    rows = -(-nb // 8) * 8
    cp = jnp.zeros((rows, d), F32).at[:nb].set(c)
    n_out = w_mod.shape[1]
    tn = 512
    out = pl.pallas_call(
        _mod_kernel,
        grid=(n_out // tn,),
        in_specs=[pl.BlockSpec((rows, d), lambda j: (0, 0)),
                  pl.BlockSpec((d, tn), lambda j: (0, j)),
                  pl.BlockSpec((1, tn), lambda j: (0, j))],
        out_specs=pl.BlockSpec((rows, tn), lambda j: (0, j)),
        out_shape=jax.ShapeDtypeStruct((rows, n_out), F32),
        compiler_params=_cparams("arbitrary"),
        name="mod",
    )(cp, w_mod, b_mod.reshape(1, n_out))
    mod = out[:nb].reshape(nb, 6, d)
    return jnp.concatenate([mod, jnp.zeros((nb, MOD_ROWS - 6, d), F32)], axis=1)


def _rope(x, cos, sinm, sinp, shift):
    return x * cos + pltpu.roll(x, LANE - shift, 1) * sinm + pltpu.roll(x, shift, 1) * sinp


def _pre_kernel(x_ref, mod_ref, gattn_ref, w1_ref, wvat_ref, gqa_ref, gka_ref, gqlat_ref, wuq_ref,
                gkvlat_ref, wuk_ref, wuvt_ref, ta_ref, tb_ref,
                qa_ref, ka_ref, vta_ref, qb_ref, kb_ref, vtb_ref, sig_ref):
    x = x_ref[...]
    u = _rms(x, gattn_ref[...]) * (1.0 + mod_ref[_SC_A:_SC_A + 1, :]) + mod_ref[_SH_A:_SH_A + 1, :]
    ub = u.astype(BF16)

    def proj(lo, hi):
        return jnp.dot(ub, w1_ref[:, lo:hi], preferred_element_type=F32)

    cos_a, sinm_a, sinp_a = ta_ref[0], ta_ref[1], ta_ref[2]
    cos_b, sinm_b, sinp_b = tb_ref[0], tb_ref[1], tb_ref[2]
    shift_a = HEAD_DIM_A // 4
    shift_b = ROPE_DIM_B // 4

    def head_a(xs, g, scale):
        ms = jnp.sum(xs * xs, axis=-1, keepdims=True) * (1.0 / HEAD_DIM_A)
        xn = xs * lax.rsqrt(ms + EPS) * g
        return (_rope(xn, cos_a, sinm_a, sinp_a, shift_a) * scale).astype(BF16)

    qa = proj(_C_QA, _C_KA)
    for h in range(N_HEADS_A):
        qa_ref[:, h * LANE:(h + 1) * LANE] = head_a(qa[:, h * LANE:(h + 1) * LANE], gqa_ref[...],
                                                    HEAD_DIM_A ** -0.5 * LOG2E)
    ka = proj(_C_KA, _C_QLAT)
    base_lane = lax.broadcasted_iota(jnp.int32, (x.shape[0], LANE), 1) == BASE_LANE
    for h in range(N_KV_A):
        kh = head_a(ka[:, h * LANE:(h + 1) * LANE], gka_ref[...], 1.0)
        ka_ref[:, h * LANE:(h + 1) * LANE] = jnp.where(base_lane, 1.0, kh).astype(BF16)
    vta = lax.dot_general(wvat_ref[...], ub, _NT, preferred_element_type=F32)
    vta_ref[0, :, :HEAD_DIM_A, :] = vta.astype(BF16).reshape(N_KV_A, HEAD_DIM_A, vta.shape[-1])
    vta_ref[0, :, HEAD_DIM_A:, :] = jnp.ones((N_KV_A, ONES_ROWS, vta.shape[-1]), BF16)

    qn = _rms(proj(_C_QLAT, _C_KVLAT), gqlat_ref[...]).astype(BF16)
    qb = jnp.dot(qn, wuq_ref[...], preferred_element_type=F32)
    scale_b = (NOPE_DIM + ROPE_DIM_B) ** -0.5 * LOG2E
    for h in range(N_HEADS_B):
        sl = slice(h * LANE, (h + 1) * LANE)
        qb_ref[:, sl] = (_rope(qb[:, sl], cos_b, sinm_b, sinp_b, shift_b) * scale_b).astype(BF16)
    kvn = _rms(proj(_C_KVLAT, _C_KPE), gkvlat_ref[...]).astype(BF16)
    kpe = _rope(proj(_C_KPE, _C_GATE), cos_b, sinm_b, sinp_b, shift_b)
    kb = jnp.dot(kvn, wuk_ref[...], preferred_element_type=F32)
    for h in range(N_HEADS_B):
        sl = slice(h * LANE, (h + 1) * LANE)
        kb_ref[:, sl] = jnp.where(base_lane, 1.0, kb[:, sl] + kpe).astype(BF16)
    vtb = lax.dot_general(wuvt_ref[...], kvn, _NT, preferred_element_type=F32)
    vtb_ref[0, :, :V_DIM_B, :] = vtb.astype(BF16).reshape(N_HEADS_B, V_DIM_B, vtb.shape[-1])
    vtb_ref[0, :, V_DIM_B:, :] = jnp.ones((N_HEADS_B, ONES_ROWS, vtb.shape[-1]), BF16)

    sig_ref[...] = jax.nn.sigmoid(proj(_C_GATE, _C_END)).astype(BF16)


def _pre(x2d, mod, wp, tables_a, tables_b, seq, tm):
    n_tok, d = x2d.shape
    n_tiles = n_tok // tm
    tiles_per_seq = seq // tm
    full = lambda a: pl.BlockSpec(a.shape, lambda i: (0,) * a.ndim)
    tok = lambda cols: pl.BlockSpec((tm, cols), lambda i: (i, 0))
    tab = pl.BlockSpec((3, tm, LANE), lambda i: (0, i % tiles_per_seq, 0))
    in_specs = [tok(d),
                pl.BlockSpec((None, MOD_ROWS, d), lambda i: (i // tiles_per_seq, 0, 0)),
                full(wp["g_attn"]), full(wp["w1"]), full(wp["wvat"]), full(wp["g_qa"]), full(wp["g_ka"]),
                full(wp["g_qlat"]), full(wp["wuq"]), full(wp["g_kvlat"]), full(wp["wuk"]), full(wp["wuvt"]),
                tab, tab]
    out_shape = [
        jax.ShapeDtypeStruct((n_tok, N_HEADS_A * LANE), BF16),
        jax.ShapeDtypeStruct((n_tok, N_KV_A * LANE), BF16),
        jax.ShapeDtypeStruct((n_tiles, N_KV_A, HEAD_DIM_A + ONES_ROWS, tm), BF16),
        jax.ShapeDtypeStruct((n_tok, N_HEADS_B * LANE), BF16),
        jax.ShapeDtypeStruct((n_tok, N_HEADS_B * LANE), BF16),
        jax.ShapeDtypeStruct((n_tiles, N_HEADS_B, V_DIM_B + ONES_ROWS, tm), BF16),
        jax.ShapeDtypeStruct((n_tok, 2 * D_MODEL), BF16),
    ]
    out_specs = [tok(N_HEADS_A * LANE), tok(N_KV_A * LANE),
                 pl.BlockSpec((1, N_KV_A, HEAD_DIM_A + ONES_ROWS, tm), lambda i: (i, 0, 0, 0)),
                 tok(N_HEADS_B * LANE), tok(N_HEADS_B * LANE),
                 pl.BlockSpec((1, N_HEADS_B, V_DIM_B + ONES_ROWS, tm), lambda i: (i, 0, 0, 0)),
                 tok(2 * D_MODEL)]
    return pl.pallas_call(
        _pre_kernel, grid=(n_tiles,), in_specs=in_specs, out_specs=out_specs, out_shape=out_shape,
        compiler_params=_cparams("arbitrary"), name="pre",
    )(x2d, mod, wp["g_attn"], wp["w1"], wp["wvat"], wp["g_qa"], wp["g_ka"], wp["g_qlat"], wp["wuq"],
      wp["g_kvlat"], wp["wuk"], wp["wuvt"], tables_a, tables_b)


def _attn_kernel(q_ref, k_ref, vt_ref, o_ref, kmax_ref, *p_bufs, n_sub, tq, tk, n_chunks, dv, unroll):
    lag = ATTN_LAG
    cols = n_sub * tq
    if n_sub == 1:
        q = q_ref[...]
    else:
        q = jnp.concatenate([q_ref[:, j * LANE:(j + 1) * LANE] for j in range(n_sub)], axis=0)

    def key_chunk(i):
        return k_ref[pl.ds(pl.multiple_of(i * tk, tk), tk), :]

    @pl.when(pl.program_id(2) == 0)
    def _():
        def norm_body(c, mx):
            kc = key_chunk(c).astype(F32)
            n2 = jnp.sum(kc * kc, axis=-1, keepdims=True)
            return jnp.maximum(mx, jnp.max(n2, axis=0, keepdims=True))

        kmax2 = lax.fori_loop(0, n_chunks, norm_body, jnp.zeros((1, 1), F32))
        kmax_ref[...] = jnp.broadcast_to(jnp.sqrt(jnp.maximum(kmax2 - 1.0, 0.0)), kmax_ref.shape)

    qt = q.astype(F32).T
    base = jnp.sqrt(jnp.sum(qt * qt, axis=0, keepdims=True)) * kmax_ref[0:1, 0:1]
    row = lax.broadcasted_iota(jnp.int32, qt.shape, 0)
    qt_aug = jnp.where(row == BASE_LANE, -base, qt).astype(BF16)

    def pv(c, p_ref):
        return jnp.dot(vt_ref[c], p_ref[...], preferred_element_type=F32)

    def trip(i, u, acc):
        acc = acc + pv(jnp.maximum(i - lag, 0), p_bufs[(u - lag) % ATTN_RING])
        s = jnp.dot(key_chunk(i), qt_aug, preferred_element_type=F32)
        p_bufs[u % ATTN_RING][...] = jnp.exp2(s).astype(BF16)
        return acc

    def body(j, acc):
        for u in range(unroll):
            acc = trip(j * unroll + u, u, acc)
        return acc

    for c in range(lag):
        p_bufs[(-1 - c) % ATTN_RING][...] = jnp.zeros((tk, cols), BF16)
    acc = lax.fori_loop(0, n_chunks // unroll, body, jnp.zeros((dv + ONES_ROWS, cols), F32))
    for c in range(n_chunks - lag, n_chunks):
        acc = acc + pv(c, p_bufs[c % ATTN_RING])

    def emit(acc):
        o = acc[:dv] / acc[dv:dv + 1]
        for j in range(n_sub):
            o_ref[j * dv:(j + 1) * dv, :] = o[:, j * tq:(j + 1) * tq].astype(BF16)

    emit(acc)
    den = acc[dv:dv + 1]
    safe = jnp.logical_and(den > DEN_MIN, den < DEN_MAX)
    n_unsafe = jnp.sum(jnp.where(safe, 0.0, 1.0))

    @pl.when(n_unsafe > 0.0)
    def _():
        def exact_body(i, carry):
            m, acc = carry
            s = lax.dot_general(key_chunk(i), q, _NT, preferred_element_type=F32)
            m_new = jnp.maximum(m, jnp.max(s, axis=0, keepdims=True))
            p = jnp.exp2(s - m_new).astype(BF16)
            return m_new, jnp.exp2(m - m_new) * acc + jnp.dot(vt_ref[i], p, preferred_element_type=F32)

        init = (jnp.full((1, cols), NEG_BIG, F32), jnp.zeros((dv + ONES_ROWS, cols), F32))
        emit(lax.fori_loop(0, n_chunks, exact_body, init)[1])


def _attention(q, k, vt, *, n_batch, seq, n_groups, n_sub, tq, dv):
    n_tok = q.shape[0]
    tk = vt.shape[-1]
    n_chunks = seq // tk
    q_tiles = seq // tq
    unroll = ATTN_UNROLL
    while n_chunks % unroll:
        unroll //= 2
    assert unroll % ATTN_RING == 0, "buffer slots must be static per trip"
    kern = functools.partial(_attn_kernel, n_sub=n_sub, tq=tq, tk=tk, n_chunks=n_chunks, dv=dv,
                             unroll=unroll)
    cols = n_sub * tq
    return pl.pallas_call(
        kern,
        grid=(n_batch, n_groups, q_tiles),
        in_specs=[pl.BlockSpec((tq, n_sub * LANE), lambda b, g, i: (b * q_tiles + i, g)),
                  pl.BlockSpec((seq, LANE), lambda b, g, i: (b, g)),
                  pl.BlockSpec((n_chunks, None, dv + ONES_ROWS, tk), lambda b, g, i: (b, g, 0, 0))],
        out_specs=pl.BlockSpec((n_sub * dv, tq), lambda b, g, i: (g, b * q_tiles + i)),
        out_shape=jax.ShapeDtypeStruct((n_groups * n_sub * dv, n_tok), BF16),
        scratch_shapes=[pltpu.VMEM((8, LANE), F32)] + [pltpu.VMEM((tk, cols), BF16)] * ATTN_RING,
        compiler_params=_cparams("arbitrary", "arbitrary", "arbitrary"),
        name="attn",
    )(q, k, vt)


def _post_kernel(x_ref, mod_ref, ota_ref, otb_ref, sig_ref, wbra_ref, wbrb_ref, wout_ref, gmoe_ref,
                 wr_ref, br_ref,
                 x1_ref, u2_ref, eid_ref, rank_ref, gate_ref, cnt_ref, carry_ref):
    i = pl.program_id(0)

    @pl.when(i == 0)
    def _():
        carry_ref[...] = jnp.zeros_like(carry_ref)

    d = D_MODEL
    ya = lax.dot_general(ota_ref[...], wbra_ref[...], _TN, preferred_element_type=F32)
    yb = lax.dot_general(otb_ref[...], wbrb_ref[...], _TN, preferred_element_type=F32)
    mixed = sig_ref[:, :d].astype(F32) * ya + sig_ref[:, d:].astype(F32) * yb
    att = jnp.dot(mixed.astype(BF16), wout_ref[...], preferred_element_type=F32)
    x1 = x_ref[...] + mod_ref[_GT_A:_GT_A + 1, :] * att
    x1_ref[...] = x1
    u2 = _rms(x1, gmoe_ref[...]) * (1.0 + mod_ref[_SC_M:_SC_M + 1, :]) + mod_ref[_SH_M:_SH_M + 1, :]
    u2_ref[...] = u2.reshape(u2_ref.shape)

    logits = jnp.dot(u2.astype(BF16), wr_ref[...], preferred_element_type=F32) + br_ref[...]
    tm = logits.shape[0]
    lane = lax.broadcasted_iota(jnp.int32, (tm, LANE), 1)
    vals = logits
    top_v, top_i = [], []
    for _ in range(TOP_K):
        mx = jnp.max(vals, axis=-1, keepdims=True)
        idx = jnp.min(jnp.where(vals == mx, lane, LANE), axis=-1, keepdims=True)
        top_v.append(mx)
        top_i.append(idx)
        vals = jnp.where(lane == idx, -jnp.inf, vals)
    ex = [jnp.exp(v - top_v[0]) for v in top_v]
    den = ex[0] + ex[1] + ex[2] + ex[3]

    onehot = [lane == idx for idx in top_i]
    cnt = sum(oh.astype(F32) for oh in onehot)
    r_i = lax.broadcasted_iota(jnp.int32, (tm, tm), 0)
    c_i = lax.broadcasted_iota(jnp.int32, (tm, tm), 1)
    lower = (c_i < r_i).astype(BF16)
    before = jnp.dot(lower, cnt.astype(BF16), preferred_element_type=F32) + carry_ref[...]
    eid = jnp.zeros((tm, LANE), jnp.int32)
    rank = jnp.zeros((tm, LANE), jnp.int32)
    gate = jnp.zeros((tm, LANE), F32)
    for k in range(TOP_K):
        rk = jnp.sum(jnp.where(onehot[k], before, 0.0), axis=-1, keepdims=True).astype(jnp.int32)
        eid = jnp.where(lane == k, top_i[k], eid)
        rank = jnp.where(lane == k, rk, rank)
        gate = jnp.where(lane == k, ex[k] / den, gate)
    eid_ref[...] = eid
    rank_ref[...] = rank
    gate_ref[...] = gate
    carry = carry_ref[...] + jnp.sum(cnt, axis=0, keepdims=True)
    carry_ref[...] = carry
    cnt_ref[...] = carry


def _post(x2d, mod, ota, otb, sig, wp, seq, tm):
    n_tok, d = x2d.shape
    n_tiles = n_tok // tm
    tiles_per_seq = seq // tm
    full = lambda a: pl.BlockSpec(a.shape, lambda i: (0,) * a.ndim)
    tok = lambda cols: pl.BlockSpec((tm, cols), lambda i: (i, 0))
    tcol = lambda rows: pl.BlockSpec((rows, tm), lambda i: (0, i))
    in_specs = [tok(d), pl.BlockSpec((None, MOD_ROWS, d), lambda i: (i // tiles_per_seq, 0, 0)),
                tcol(ota.shape[0]), tcol(otb.shape[0]), tok(2 * d),
                full(wp["wbra"]), full(wp["wbrb"]), full(wp["wout"]), full(wp["g_moe"]),
                full(wp["wr"]), full(wp["br"])]
    out_shape = [jax.ShapeDtypeStruct((n_tok, d), F32), jax.ShapeDtypeStruct((n_tok, d // LANE, LANE), F32),
                 jax.ShapeDtypeStruct((n_tok, LANE), jnp.int32), jax.ShapeDtypeStruct((n_tok, LANE), jnp.int32),
                 jax.ShapeDtypeStruct((n_tok, LANE), F32), jax.ShapeDtypeStruct((1, LANE), F32)]
    out_specs = [tok(d), pl.BlockSpec((tm, d // LANE, LANE), lambda i: (i, 0, 0)),
                 tok(LANE), tok(LANE), tok(LANE), pl.BlockSpec((1, LANE), lambda i: (0, 0))]
    return pl.pallas_call(
        _post_kernel, grid=(n_tiles,), in_specs=in_specs, out_specs=out_specs, out_shape=out_shape,
        scratch_shapes=[pltpu.VMEM((1, LANE), F32)],
        compiler_params=_cparams("arbitrary"), name="post",
    )(x2d, mod, ota, otb, sig, wp["wbra"], wp["wbrb"], wp["wout"], wp["g_moe"], wp["wr"], wp["br"])


def _row_copy(src, dst, sem):
    return pltpu.make_async_copy(src, dst, sem)


def _dispatch_kernel(pad_end_ref, dest_ref, u_ref, xs_ref, zero_ref, sem, zero_sem, *, tm):
    @pl.when(pl.program_id(0) == 0)
    def _():
        zero_ref[...] = jnp.zeros_like(zero_ref)

        def tail_copy(e):
            start = pl.multiple_of(pad_end_ref[e] - EXPERT_BLOCK, EXPERT_BLOCK)
            return pltpu.make_async_copy(zero_ref, xs_ref.at[pl.ds(start, EXPERT_BLOCK)], zero_sem)

        def nonempty(e):
            return pad_end_ref[e] > (pad_end_ref[e - 1] if e else 0)

        for e in range(N_EXPERTS):
            pl.when(nonempty(e))(lambda e=e: tail_copy(e).start())
        for e in range(N_EXPERTS):
            pl.when(nonempty(e))(lambda e=e: tail_copy(e).wait())

        def unused_copy(b):
            start = pl.multiple_of(b * EXPERT_BLOCK, EXPERT_BLOCK)
            return pltpu.make_async_copy(zero_ref, xs_ref.at[pl.ds(start, EXPERT_BLOCK)], zero_sem)

        first_unused = pad_end_ref[N_EXPERTS - 1] // EXPERT_BLOCK
        n_blocks = xs_ref.shape[0] // EXPERT_BLOCK
        lax.fori_loop(first_unused, n_blocks, lambda b, c: (unused_copy(b).start(), c)[1], 0)
        lax.fori_loop(first_unused, n_blocks, lambda b, c: (unused_copy(b).wait(), c)[1], 0)

    def issue(r, c):
        for k in range(TOP_K):
            d = dest_ref[0, 0, r * TOP_K + k]
            _row_copy(u_ref.at[pl.ds(r, 1)], xs_ref.at[pl.ds(d, 1)], sem).start(priority=k % 2)
        return c

    lax.fori_loop(0, tm, issue, 0, unroll=ROW_LOOP_UNROLL)

    def drain(r, c):
        for k in range(TOP_K):
            _row_copy(u_ref.at[pl.ds(0, 1)], xs_ref.at[pl.ds(0, 1)], sem).wait()
        return c

    lax.fori_loop(0, tm, drain, 0, unroll=ROW_LOOP_UNROLL)


def _dispatch(u2, dest, pad_end, cap, tm):
    n_tok, row = u2.shape[0], u2.shape[1:]
    n_tiles = n_tok // tm
    dest3 = dest.reshape(n_tiles, 1, tm * TOP_K)
    grid_spec = pltpu.PrefetchScalarGridSpec(
        num_scalar_prefetch=1,
        grid=(n_tiles,),
        in_specs=[pl.BlockSpec((1, 1, tm * TOP_K), lambda i, pe: (i, 0, 0), memory_space=pltpu.SMEM),
                  pl.BlockSpec((tm,) + row, lambda i, pe: (i, 0, 0))],
        out_specs=pl.BlockSpec(memory_space=pl.ANY),
        scratch_shapes=[pltpu.VMEM((EXPERT_BLOCK,) + row, u2.dtype), pltpu.SemaphoreType.DMA(()),
                        pltpu.SemaphoreType.DMA(())],
    )
    return pl.pallas_call(
        functools.partial(_dispatch_kernel, tm=tm), grid_spec=grid_spec,
        out_shape=jax.ShapeDtypeStruct((cap,) + row, u2.dtype),
        compiler_params=_cparams("arbitrary"), name="dispatch",
    )(pad_end, dest3, u2)


def _expert_kernel(be_ref, valid_ref, xs_ref, wgu_ref, bgu_ref, wd_ref, bd_ref, y_ref, wgu_bf, wd_bf):
    b = pl.program_id(0)

    @pl.when(jnp.logical_or(b == 0, be_ref[b] != be_ref[jnp.maximum(b - 1, 0)]))
    def _():
        wgu_bf[...] = wgu_ref[0].astype(BF16)
        wd_bf[...] = wd_ref[0].astype(BF16)

    @pl.when(valid_ref[b] == 1)
    def _():
        x = xs_ref[...].reshape(EXPERT_BLOCK, D_MODEL).astype(BF16)
        h = jnp.dot(x, wgu_bf[...], preferred_element_type=F32) + bgu_ref[0]
        hg = jnp.minimum(h[:, :D_FF], SWIGLU_LIMIT)
        hl = jnp.clip(h[:, D_FF:], -SWIGLU_LIMIT, SWIGLU_LIMIT)
        a = hg * jax.nn.sigmoid(SWIGLU_ALPHA * hg) * (hl + 1.0)
        y = jnp.dot(a.astype(BF16), wd_bf[...], preferred_element_type=F32) + bd_ref[0]
        y_ref[...] = y.reshape(y_ref.shape)

    @pl.when(valid_ref[b] == 0)
    def _():
        y_ref[...] = jnp.zeros_like(y_ref)


def _experts(xs, block_e, block_valid, wp):
    cap, d = xs.shape[0], D_MODEL
    n_blocks = cap // EXPERT_BLOCK
    grid_spec = pltpu.PrefetchScalarGridSpec(
        num_scalar_prefetch=2,
        grid=(n_blocks,),
        in_specs=[pl.BlockSpec((EXPERT_BLOCK,) + xs.shape[1:], lambda b, be, bv: (jnp.where(bv[b] == 1, b, 0), 0, 0)),
                  pl.BlockSpec((1, d, 2 * D_FF), lambda b, be, bv: (be[b], 0, 0)),
                  pl.BlockSpec((1, 1, 2 * D_FF), lambda b, be, bv: (be[b], 0, 0)),
                  pl.BlockSpec((1, D_FF, d), lambda b, be, bv: (be[b], 0, 0)),
                  pl.BlockSpec((1, 1, d), lambda b, be, bv: (be[b], 0, 0))],
        out_specs=pl.BlockSpec((EXPERT_BLOCK,) + xs.shape[1:], lambda b, be, bv: (b, 0, 0)),
        scratch_shapes=[pltpu.VMEM((d, 2 * D_FF), BF16), pltpu.VMEM((D_FF, d), BF16)],
    )
    return pl.pallas_call(
        _expert_kernel, grid_spec=grid_spec,
        out_shape=jax.ShapeDtypeStruct(xs.shape, F32),
        compiler_params=_cparams("arbitrary"), name="experts",
    )(block_e, block_valid, xs, wp["wgu"], wp["bgu"], wp["wd"], wp["bd"])


def _combine_kernel(dest_ref, dest_next_ref, x1_ref, gate_ref, mod_ref, gfin_ref, y_hbm, o_ref, buf, sem, *, tm):
    i = pl.program_id(0)
    slot = i % 2

    def gather(dref, s):
        def issue(r, c):
            for k in range(TOP_K):
                d = dref[0, 0, r * TOP_K + k]
                _row_copy(y_hbm.at[pl.ds(d, 1)], buf.at[s, k, pl.ds(r, 1)], sem.at[s]).start(priority=k % 2)
            return c

        lax.fori_loop(0, tm, issue, 0, unroll=ROW_LOOP_UNROLL)

    @pl.when(i == 0)
    def _():
        gather(dest_ref, slot)

    @pl.when(i + 1 < pl.num_programs(0))
    def _():
        gather(dest_next_ref, 1 - slot)

    def drain(r, c):
        for k in range(TOP_K):
            _row_copy(y_hbm.at[pl.ds(0, 1)], buf.at[slot, k, pl.ds(0, 1)], sem.at[slot]).wait()
        return c

    lax.fori_loop(0, tm, drain, 0, unroll=ROW_LOOP_UNROLL)
    gate = gate_ref[...]
    rows = lambda k: buf[slot, k].reshape(tm, D_MODEL)
    y = gate[:, 0:1] * rows(0)
    for k in range(1, TOP_K):
        y = y + gate[:, k:k + 1] * rows(k)
    x2 = x1_ref[...] + mod_ref[_GT_M:_GT_M + 1, :] * y
    o_ref[...] = _rms(x2, gfin_ref[...])


def _combine(x1, gate, mod, g_final, yb, dest, seq, tm):
    n_tok, d = x1.shape
    n_tiles = n_tok // tm
    tiles_per_seq = seq // tm
    dest3 = dest.reshape(n_tiles, 1, tm * TOP_K)
    return pl.pallas_call(
        functools.partial(_combine_kernel, tm=tm),
        grid=(n_tiles,),
        in_specs=[pl.BlockSpec((1, 1, tm * TOP_K), lambda i: (i, 0, 0), memory_space=pltpu.SMEM),
                  pl.BlockSpec((1, 1, tm * TOP_K), lambda i: (jnp.minimum(i + 1, n_tiles - 1), 0, 0),
                               memory_space=pltpu.SMEM),
                  pl.BlockSpec((tm, d), lambda i: (i, 0)),
                  pl.BlockSpec((tm, LANE), lambda i: (i, 0)),
                  pl.BlockSpec((None, MOD_ROWS, d), lambda i: (i // tiles_per_seq, 0, 0)),
                  pl.BlockSpec((1, d), lambda i: (0, 0)),
                  pl.BlockSpec(memory_space=pl.ANY)],
        out_specs=pl.BlockSpec((tm, d), lambda i: (i, 0)),
        out_shape=jax.ShapeDtypeStruct((n_tok, d), F32),
        scratch_shapes=[pltpu.VMEM((2, TOP_K, tm) + yb.shape[1:], F32), pltpu.SemaphoreType.DMA((2,))],
        compiler_params=_cparams("arbitrary"), name="combine",
    )(dest3, dest3, x1, gate, mod, g_final, yb)


def _slot_cols(w, n_heads, width, offset=0):
    k = w.shape[0]
    w3 = w.reshape(k, n_heads, width)
    out = jnp.pad(w3, ((0, 0), (0, 0), (offset, LANE - offset - width)))
    return out.reshape(k, n_heads * LANE)


def _prep_weights(g_attn, g_moe, w_in, g_qn, g_kn, g_qlat, w_uq, g_kvlat, w_ukv, w_br_a, w_br_b, w_out,
                  w_router, b_router, w_gu, b_gu, w_down, b_down):
    d = D_MODEL
    wa = N_HEADS_A * HEAD_DIM_A
    wk = N_KV_A * HEAD_DIM_A
    c0 = 0
    w_qa = w_in[:, c0:c0 + wa]; c0 += wa
    w_ka = w_in[:, c0:c0 + wk]; c0 += wk
    w_va = w_in[:, c0:c0 + wk]; c0 += wk
    w_ql = w_in[:, c0:c0 + Q_LORA]; c0 += Q_LORA
    w_kvl = w_in[:, c0:c0 + KV_LORA]; c0 += KV_LORA
    w_kpe = w_in[:, c0:c0 + ROPE_DIM_B]; c0 += ROPE_DIM_B
    w_gate = w_in[:, c0:]
    w1 = jnp.concatenate([
        _slot_cols(w_qa, N_HEADS_A, HEAD_DIM_A), _slot_cols(w_ka, N_KV_A, HEAD_DIM_A), w_ql, w_kvl,
        _slot_cols(w_kpe, 1, ROPE_DIM_B, NOPE_DIM), w_gate], axis=1).astype(BF16)
    ukv = w_ukv.reshape(KV_LORA, N_HEADS_B, NOPE_DIM + V_DIM_B)
    w_uk = ukv[:, :, :NOPE_DIM].reshape(KV_LORA, N_HEADS_B * NOPE_DIM)
    w_uv = ukv[:, :, NOPE_DIM:].reshape(KV_LORA, N_HEADS_B * V_DIM_B)
    pad_gain = lambda g: jnp.zeros((1, LANE), F32).at[0, :g.shape[0]].set(g)
    return {
        "g_attn": g_attn.reshape(1, d), "g_moe": g_moe.reshape(1, d),
        "w1": w1, "wvat": w_va.T.astype(BF16),
        "g_qa": pad_gain(g_qn), "g_ka": pad_gain(g_kn),
        "g_qlat": g_qlat.reshape(1, Q_LORA), "g_kvlat": g_kvlat.reshape(1, KV_LORA),
        "wuq": _slot_cols(w_uq, N_HEADS_B, NOPE_DIM + ROPE_DIM_B).astype(BF16),
        "wuk": _slot_cols(w_uk, N_HEADS_B, NOPE_DIM).astype(BF16),
        "wuvt": w_uv.T.astype(BF16),
        "wbra": w_br_a.astype(BF16), "wbrb": w_br_b.astype(BF16), "wout": w_out.astype(BF16),
        "wr": jnp.zeros((d, LANE), F32).at[:, :N_EXPERTS].set(w_router).astype(BF16),
        "br": jnp.full((1, LANE), NEG_BIG, F32).at[0, :N_EXPERTS].set(b_router),
        "wgu": w_gu, "bgu": b_gu.reshape(N_EXPERTS, 1, 2 * D_FF),
        "wd": w_down, "bd": b_down.reshape(N_EXPERTS, 1, d),
    }


def _rope_tables(seq, rot_dim, offset):
    half = rot_dim // 2
    quarter = half // 2
    t = jnp.arange(seq, dtype=jnp.int32)
    row = (t // GRID_W).astype(F32)
    col = (t % GRID_W).astype(F32)
    inv = ROPE_THETA ** (-jnp.arange(0, half, 2, dtype=F32) / half)
    ar, ac = row[:, None] * inv, col[:, None] * inv
    z = jnp.zeros((seq, quarter), F32)
    cr, sr, cc, sc = jnp.cos(ar), jnp.sin(ar), jnp.cos(ac), jnp.sin(ac)

    def place(pieces, fill):
        lo = jnp.full((seq, offset), fill, F32)
        hi = jnp.full((seq, LANE - offset - rot_dim), fill, F32)
        return jnp.concatenate([lo] + pieces + [hi], axis=1)

    return jnp.stack([place([cr, cr, cc, cc], 1.0), place([-sr, z, -sc, z], 0.0), place([z, sr, z, sc], 0.0)])


def _trunk(x, mod, wp, g_final, ta, tb):
    n_batch, seq, d = x.shape
    n_tok = n_batch * seq
    x2d = x.reshape(n_tok, d)
    tm = min(256, seq)

    qa, ka, vta, qb, kb, vtb, sig = _pre(x2d, mod, wp, ta, tb, seq, tm)
    cols = ATTN_COLS
    ota = _attention(qa, ka, vta, n_batch=n_batch, seq=seq, n_groups=N_KV_A, n_sub=GROUP_A,
                     tq=min(cols // GROUP_A, seq), dv=HEAD_DIM_A)
    otb = _attention(qb, kb, vtb, n_batch=n_batch, seq=seq, n_groups=N_HEADS_B, n_sub=1,
                     tq=min(cols, seq), dv=V_DIM_B)
    x1, u2, eid, rank, gate, cnt = _post(x2d, mod, ota, otb, sig, wp, seq, tm)

    n_assign = n_tok * TOP_K
    n_blocks = -(-n_assign // EXPERT_BLOCK) + N_EXPERTS
    cap = n_blocks * EXPERT_BLOCK
    counts = cnt[0, :N_EXPERTS].astype(jnp.int32)
    padded = (counts + EXPERT_BLOCK - 1) // EXPERT_BLOCK * EXPERT_BLOCK
    pad_end = jnp.cumsum(padded)
    pad_start = pad_end - padded
    experts = jnp.arange(N_EXPERTS, dtype=jnp.int32)
    start = jnp.sum(jnp.where(eid[:, :TOP_K, None] == experts, pad_start.astype(jnp.int32), 0), axis=-1)
    dest = start + rank[:, :TOP_K]
    block_lo = jnp.arange(n_blocks, dtype=jnp.int32) * EXPERT_BLOCK
    block_e = jnp.minimum(jnp.sum(block_lo[:, None] >= pad_end[None, :], axis=1), N_EXPERTS - 1).astype(jnp.int32)
    block_valid = (block_lo < pad_end[-1]).astype(jnp.int32)

    xs = _dispatch(u2, dest, pad_end.astype(jnp.int32), cap, min(DISPATCH_TILE, n_tok))
    yb = _experts(xs, block_e, block_valid, wp)
    out = _combine(x1, gate, mod, g_final.reshape(1, d), yb, dest, seq, tm)
    return out.reshape(n_batch, seq, d)


def kernel(x_prompt, x_sample, c_prompt, c_sample, w_mod, b_mod, g_attn, g_moe, w_in, g_qn, g_kn, g_qlat,
           w_uq, g_kvlat, w_ukv, w_br_a, w_br_b, w_out, w_router, b_router, w_gu, b_gu, w_down, b_down,
           g_final):
    assert w_mod.shape[0] == 1, "single-layer trunk"
    wp = _prep_weights(g_attn[0], g_moe[0], w_in[0], g_qn[0], g_kn[0], g_qlat[0], w_uq[0], g_kvlat[0],
                       w_ukv[0], w_br_a[0], w_br_b[0], w_out[0], w_router[0], b_router[0], w_gu[0],
                       b_gu[0], w_down[0], b_down[0])
    nb_p = c_prompt.shape[0]
    mod = _modulation(jnp.concatenate([c_prompt, c_sample], axis=0), w_mod[0], b_mod[0])
    max_seq = max(x_prompt.shape[1], x_sample.shape[1])
    ta = _rope_tables(max_seq, HEAD_DIM_A, 0)
    tb = _rope_tables(max_seq, ROPE_DIM_B, NOPE_DIM)
    y_prompt = _trunk(x_prompt, mod[:nb_p], wp, g_final, ta, tb)
    y_sample = _trunk(x_sample, mod[nb_p:], wp, g_final, ta, tb)
    return (y_prompt, y_sample)
```

```python
import functools

import numpy as np
import jax
import jax.numpy as jnp
from jax import lax
from jax.experimental import pallas as pl
from jax.experimental.pallas import tpu as pltpu

F32 = jnp.float32
BF16 = jnp.bfloat16

D_MODEL = 1024
GRID_W = 64
ROPE_THETA = 10000.0
EPS = 1e-6
N_HEADS_A = 8
N_KV_A = 2
GROUP_A = N_HEADS_A // N_KV_A
HEAD_DIM_A = 64
N_HEADS_B = 8
Q_LORA = 768
KV_LORA = 256
NOPE_DIM = 64
ROPE_DIM_B = 32
V_DIM_B = 64
N_EXPERTS = 32
TOP_K = 4
D_FF = D_MODEL
SWIGLU_ALPHA = 1.702
SWIGLU_LIMIT = 7.0
EXPERT_BLOCK = 256

LANE = 128
BF16_SUBLANES = 16
NEG_BIG = -1e30
LOG2E = 1.4426950408889634
ONES_ROWS = BF16_SUBLANES
VMEM_LIMIT = 52 * 1024 * 1024
ATTN_LAG = 2
ATTN_RING = 4
ATTN_COLS = 1024
ATTN_UNROLL = 32
DISPATCH_TILE = 512
ROW_LOOP_UNROLL = 4
BASE_LANE = LANE - 1
DEN_MIN, DEN_MAX = 1e-30, 1e30

_C_QA = 0
_C_KA = _C_QA + N_HEADS_A * LANE
_C_QLAT = _C_KA + N_KV_A * LANE
_C_KVLAT = _C_QLAT + Q_LORA
_C_KPE = _C_KVLAT + KV_LORA
_C_GATE = _C_KPE + LANE
_C_END = _C_GATE + 2 * D_MODEL

_SH_A, _SC_A, _GT_A, _SH_M, _SC_M, _GT_M = range(6)
MOD_ROWS = 8

_NT = (((1,), (1,)), ((), ()))
_TN = (((0,), (0,)), ((), ()))


def _cparams(*sem):
    return pltpu.CompilerParams(dimension_semantics=sem, vmem_limit_bytes=VMEM_LIMIT)


def _rms(x, g):
    return x * lax.rsqrt(jnp.mean(x * x, axis=-1, keepdims=True) + EPS) * g


def _mod_kernel(c_ref, w_ref, b_ref, o_ref):
    c = c_ref[...]
    s = c * jax.nn.sigmoid(c)
    o_ref[...] = jnp.dot(s.astype(BF16), w_ref[...].astype(BF16), preferred_element_type=F32) + b_ref[...]


def _modulation(c, w_mod, b_mod):
    nb, d = c.shape
    rows = -(-nb // 8) * 8
    cp = jnp.zeros((rows, d), F32).at[:nb].set(c)
    n_out = w_mod.shape[1]
    tn = 512
    out = pl.pallas_call(
        _mod_kernel,
        grid=(n_out // tn,),
        in_specs=[pl.BlockSpec((rows, d), lambda j: (0, 0)),
                  pl.BlockSpec((d, tn), lambda j: (0, j)),
                  pl.BlockSpec((1, tn), lambda j: (0, j))],
        out_specs=pl.BlockSpec((rows, tn), lambda j: (0, j)),
        out_shape=jax.ShapeDtypeStruct((rows, n_out), F32),
        compiler_params=_cparams("arbitrary"),
        name="mod",
    )(cp, w_mod, b_mod.reshape(1, n_out))
    mod = out[:nb].reshape(nb, 6, d)
    return jnp.concatenate([mod, jnp.zeros((nb, MOD_ROWS - 6, d), F32)], axis=1)


def _rope(x, cos, sinm, sinp, shift):
    return x * cos + pltpu.roll(x, LANE - shift, 1) * sinm + pltpu.roll(x, shift, 1) * sinp


def _pre_kernel(x_ref, mod_ref, gattn_ref, w1_ref, wvat_ref, gqa_ref, gka_ref, gqlat_ref, wuq_ref,
                gkvlat_ref, wuk_ref, wuvt_ref, rrow_ref, rcol_ref,
                qa_ref, ka_ref, vta_ref, qb_ref, kb_ref, vtb_ref, sig_ref):
    x = x_ref[...]
    u = _rms(x, gattn_ref[...]) * (1.0 + mod_ref[_SC_A:_SC_A + 1, :]) + mod_ref[_SH_A:_SH_A + 1, :]
    ub = u.astype(BF16)

    def proj(lo, hi):
        return jnp.dot(ub, w1_ref[:, lo:hi], preferred_element_type=F32)

    rcol = rcol_ref[...]
    tab = jnp.concatenate([rrow_ref[r:r + 1, :] + rcol for r in range(rrow_ref.shape[0])], axis=0)
    cos_a, sinm_a, sinp_a, cos_b, sinm_b, sinp_b = (tab[:, j * LANE:(j + 1) * LANE] for j in range(6))
    shift_a = HEAD_DIM_A // 4
    shift_b = ROPE_DIM_B // 4

    def head_a(xs, g, scale):
        ms = jnp.sum(xs * xs, axis=-1, keepdims=True) * (1.0 / HEAD_DIM_A)
        xn = xs * lax.rsqrt(ms + EPS) * g
        return (_rope(xn, cos_a, sinm_a, sinp_a, shift_a) * scale).astype(BF16)

    qa = proj(_C_QA, _C_KA)
    for h in range(N_HEADS_A):
        qa_ref[:, h * LANE:(h + 1) * LANE] = head_a(qa[:, h * LANE:(h + 1) * LANE], gqa_ref[...],
                                                    HEAD_DIM_A ** -0.5 * LOG2E)
    ka = proj(_C_KA, _C_QLAT)
    base_lane = lax.broadcasted_iota(jnp.int32, (x.shape[0], LANE), 1) == BASE_LANE
    for h in range(N_KV_A):
        kh = head_a(ka[:, h * LANE:(h + 1) * LANE], gka_ref[...], 1.0)
        ka_ref[:, h * LANE:(h + 1) * LANE] = jnp.where(base_lane, 1.0, kh).astype(BF16)
    vta = lax.dot_general(wvat_ref[...], ub, _NT, preferred_element_type=F32)
    vta_ref[0, :, :HEAD_DIM_A, :] = vta.astype(BF16).reshape(N_KV_A, HEAD_DIM_A, vta.shape[-1])
    vta_ref[0, :, HEAD_DIM_A:, :] = jnp.ones((N_KV_A, ONES_ROWS, vta.shape[-1]), BF16)

    qn = _rms(proj(_C_QLAT, _C_KVLAT), gqlat_ref[...]).astype(BF16)
    qb = jnp.dot(qn, wuq_ref[...], preferred_element_type=F32)
    scale_b = (NOPE_DIM + ROPE_DIM_B) ** -0.5 * LOG2E
    for h in range(N_HEADS_B):
        sl = slice(h * LANE, (h + 1) * LANE)
        qb_ref[:, sl] = (_rope(qb[:, sl], cos_b, sinm_b, sinp_b, shift_b) * scale_b).astype(BF16)
    kvn = _rms(proj(_C_KVLAT, _C_KPE), gkvlat_ref[...]).astype(BF16)
    kpe = _rope(proj(_C_KPE, _C_GATE), cos_b, sinm_b, sinp_b, shift_b)
    kb = jnp.dot(kvn, wuk_ref[...], preferred_element_type=F32)
    for h in range(N_HEADS_B):
        sl = slice(h * LANE, (h + 1) * LANE)
        kb_ref[:, sl] = jnp.where(base_lane, 1.0, kb[:, sl] + kpe).astype(BF16)
    vtb = lax.dot_general(wuvt_ref[...], kvn, _NT, preferred_element_type=F32)
    vtb_ref[0, :, :V_DIM_B, :] = vtb.astype(BF16).reshape(N_HEADS_B, V_DIM_B, vtb.shape[-1])
    vtb_ref[0, :, V_DIM_B:, :] = jnp.ones((N_HEADS_B, ONES_ROWS, vtb.shape[-1]), BF16)

    sig_ref[...] = jax.nn.sigmoid(proj(_C_GATE, _C_END)).astype(BF16)


def _pre(x2d, mod, wp, rope_row, rope_col, seq, tm):
    n_tok, d = x2d.shape
    n_tiles = n_tok // tm
    tiles_per_seq = seq // tm
    assert tm % GRID_W == 0, "a token tile covers whole grid rows"
    rows_per_tile = tm // GRID_W
    rope_row = rope_row[:seq // GRID_W].reshape(tiles_per_seq, rows_per_tile, rope_row.shape[-1])
    full = lambda a: pl.BlockSpec(a.shape, lambda i: (0,) * a.ndim)
    tok = lambda cols: pl.BlockSpec((tm, cols), lambda i: (i, 0))
    in_specs = [tok(d),
                pl.BlockSpec((None, MOD_ROWS, d), lambda i: (i // tiles_per_seq, 0, 0)),
                full(wp["g_attn"]), full(wp["w1"]), full(wp["wvat"]), full(wp["g_qa"]), full(wp["g_ka"]),
                full(wp["g_qlat"]), full(wp["wuq"]), full(wp["g_kvlat"]), full(wp["wuk"]), full(wp["wuvt"]),
                pl.BlockSpec((None, rows_per_tile, rope_row.shape[-1]), lambda i: (i % tiles_per_seq, 0, 0)),
                full(rope_col)]
    out_shape = [
        jax.ShapeDtypeStruct((n_tok, N_HEADS_A * LANE), BF16),
        jax.ShapeDtypeStruct((n_tok, N_KV_A * LANE), BF16),
        jax.ShapeDtypeStruct((n_tiles, N_KV_A, HEAD_DIM_A + ONES_ROWS, tm), BF16),
        jax.ShapeDtypeStruct((n_tok, N_HEADS_B * LANE), BF16),
        jax.ShapeDtypeStruct((n_tok, N_HEADS_B * LANE), BF16),
        jax.ShapeDtypeStruct((n_tiles, N_HEADS_B, V_DIM_B + ONES_ROWS, tm), BF16),
        jax.ShapeDtypeStruct((n_tok, 2 * D_MODEL), BF16),
    ]
    out_specs = [tok(N_HEADS_A * LANE), tok(N_KV_A * LANE),
                 pl.BlockSpec((1, N_KV_A, HEAD_DIM_A + ONES_ROWS, tm), lambda i: (i, 0, 0, 0)),
                 tok(N_HEADS_B * LANE), tok(N_HEADS_B * LANE),
                 pl.BlockSpec((1, N_HEADS_B, V_DIM_B + ONES_ROWS, tm), lambda i: (i, 0, 0, 0)),
                 tok(2 * D_MODEL)]
    return pl.pallas_call(
        _pre_kernel, grid=(n_tiles,), in_specs=in_specs, out_specs=out_specs, out_shape=out_shape,
        compiler_params=_cparams("arbitrary"), name="pre",
    )(x2d, mod, wp["g_attn"], wp["w1"], wp["wvat"], wp["g_qa"], wp["g_ka"], wp["g_qlat"], wp["wuq"],
      wp["g_kvlat"], wp["wuk"], wp["wuvt"], rope_row, rope_col)


def _attn_kernel(q_ref, k_ref, vt_ref, o_ref, kmax_ref, *p_bufs, n_sub, tq, tk, n_chunks, dv, unroll):
    lag = ATTN_LAG
    cols = n_sub * tq
    if n_sub == 1:
        q = q_ref[...]
    else:
        q = jnp.concatenate([q_ref[:, j * LANE:(j + 1) * LANE] for j in range(n_sub)], axis=0)

    def key_chunk(i):
        return k_ref[pl.ds(pl.multiple_of(i * tk, tk), tk), :]

    @pl.when(pl.program_id(2) == 0)
    def _():
        def norm_body(c, mx):
            kc = key_chunk(c).astype(F32)
            n2 = jnp.sum(kc * kc, axis=-1, keepdims=True)
            return jnp.maximum(mx, jnp.max(n2, axis=0, keepdims=True))

        kmax2 = lax.fori_loop(0, n_chunks, norm_body, jnp.zeros((1, 1), F32))
        kmax_ref[...] = jnp.broadcast_to(jnp.sqrt(jnp.maximum(kmax2 - 1.0, 0.0)), kmax_ref.shape)

    qt = q.astype(F32).T
    base = jnp.sqrt(jnp.sum(qt * qt, axis=0, keepdims=True)) * kmax_ref[0:1, 0:1]
    row = lax.broadcasted_iota(jnp.int32, qt.shape, 0)
    qt_aug = jnp.where(row == BASE_LANE, -base, qt).astype(BF16)

    def pv(c, p_ref):
        return jnp.dot(vt_ref[c], p_ref[...], preferred_element_type=F32)

    def trip(i, u, acc):
        acc = acc + pv(jnp.maximum(i - lag, 0), p_bufs[(u - lag) % ATTN_RING])
        s = jnp.dot(key_chunk(i), qt_aug, preferred_element_type=F32)
        p_bufs[u % ATTN_RING][...] = jnp.exp2(s).astype(BF16)
        return acc

    def body(j, acc):
        for u in range(unroll):
            acc = trip(j * unroll + u, u, acc)
        return acc

    for c in range(lag):
        p_bufs[(-1 - c) % ATTN_RING][...] = jnp.zeros((tk, cols), BF16)
    acc = lax.fori_loop(0, n_chunks // unroll, body, jnp.zeros((dv + ONES_ROWS, cols), F32))
    for c in range(n_chunks - lag, n_chunks):
        acc = acc + pv(c, p_bufs[c % ATTN_RING])

    def emit(acc):
        o = acc[:dv] / acc[dv:dv + 1]
        for j in range(n_sub):
            o_ref[j * dv:(j + 1) * dv, :] = o[:, j * tq:(j + 1) * tq].astype(BF16)

    emit(acc)
    den = acc[dv:dv + 1]
    safe = jnp.logical_and(den > DEN_MIN, den < DEN_MAX)
    n_unsafe = jnp.sum(jnp.where(safe, 0.0, 1.0))

    @pl.when(n_unsafe > 0.0)
    def _():
        def exact_body(i, carry):
            m, acc = carry
            s = lax.dot_general(key_chunk(i), q, _NT, preferred_element_type=F32)
            m_new = jnp.maximum(m, jnp.max(s, axis=0, keepdims=True))
            p = jnp.exp2(s - m_new).astype(BF16)
            return m_new, jnp.exp2(m - m_new) * acc + jnp.dot(vt_ref[i], p, preferred_element_type=F32)

        init = (jnp.full((1, cols), NEG_BIG, F32), jnp.zeros((dv + ONES_ROWS, cols), F32))
        emit(lax.fori_loop(0, n_chunks, exact_body, init)[1])


def _attention(q, k, vt, *, n_batch, seq, n_groups, n_sub, tq, dv):
    n_tok = q.shape[0]
    tk = vt.shape[-1]
    n_chunks = seq // tk
    q_tiles = seq // tq
    unroll = ATTN_UNROLL
    while n_chunks % unroll:
        unroll //= 2
    assert unroll % ATTN_RING == 0, "buffer slots must be static per trip"
    kern = functools.partial(_attn_kernel, n_sub=n_sub, tq=tq, tk=tk, n_chunks=n_chunks, dv=dv,
                             unroll=unroll)
    cols = n_sub * tq
    return pl.pallas_call(
        kern,
        grid=(n_batch, n_groups, q_tiles),
        in_specs=[pl.BlockSpec((tq, n_sub * LANE), lambda b, g, i: (b * q_tiles + i, g)),
                  pl.BlockSpec((seq, LANE), lambda b, g, i: (b, g)),
                  pl.BlockSpec((n_chunks, None, dv + ONES_ROWS, tk), lambda b, g, i: (b, g, 0, 0))],
        out_specs=pl.BlockSpec((n_sub * dv, tq), lambda b, g, i: (g, b * q_tiles + i)),
        out_shape=jax.ShapeDtypeStruct((n_groups * n_sub * dv, n_tok), BF16),
        scratch_shapes=[pltpu.VMEM((8, LANE), F32)] + [pltpu.VMEM((tk, cols), BF16)] * ATTN_RING,
        compiler_params=_cparams("arbitrary", "arbitrary", "arbitrary"),
        name="attn",
    )(q, k, vt)


def _post_kernel(x_ref, mod_ref, ota_ref, otb_ref, sig_ref, wbra_ref, wbrb_ref, wout_ref, gmoe_ref,
                 wr_ref, br_ref,
                 x1_ref, u2_ref, eid_ref, rank_ref, gate_ref, cnt_ref, carry_ref):
    i = pl.program_id(0)

    @pl.when(i == 0)
    def _():
        carry_ref[...] = jnp.zeros_like(carry_ref)

    d = D_MODEL
    ya = lax.dot_general(ota_ref[...], wbra_ref[...], _TN, preferred_element_type=F32)
    yb = lax.dot_general(otb_ref[...], wbrb_ref[...], _TN, preferred_element_type=F32)
    mixed = sig_ref[:, :d].astype(F32) * ya + sig_ref[:, d:].astype(F32) * yb
    att = jnp.dot(mixed.astype(BF16), wout_ref[...], preferred_element_type=F32)
    x1 = x_ref[...] + mod_ref[_GT_A:_GT_A + 1, :] * att
    x1_ref[...] = x1
    u2 = _rms(x1, gmoe_ref[...]) * (1.0 + mod_ref[_SC_M:_SC_M + 1, :]) + mod_ref[_SH_M:_SH_M + 1, :]
    u2_ref[...] = u2.reshape(u2_ref.shape)

    logits = jnp.dot(u2.astype(BF16), wr_ref[...], preferred_element_type=F32) + br_ref[...]
    tm = logits.shape[0]
    lane = lax.broadcasted_iota(jnp.int32, (tm, LANE), 1)
    vals = logits
    top_v, top_i = [], []
    for _ in range(TOP_K):
        mx = jnp.max(vals, axis=-1, keepdims=True)
        idx = jnp.min(jnp.where(vals == mx, lane, LANE), axis=-1, keepdims=True)
        top_v.append(mx)
        top_i.append(idx)
        vals = jnp.where(lane == idx, -jnp.inf, vals)
    ex = [jnp.exp(v - top_v[0]) for v in top_v]
    den = ex[0] + ex[1] + ex[2] + ex[3]

    onehot = [lane == idx for idx in top_i]
    cnt = sum(oh.astype(F32) for oh in onehot)
    r_i = lax.broadcasted_iota(jnp.int32, (tm, tm), 0)
    c_i = lax.broadcasted_iota(jnp.int32, (tm, tm), 1)
    lower = (c_i < r_i).astype(BF16)
    before = jnp.dot(lower, cnt.astype(BF16), preferred_element_type=F32) + carry_ref[...]
    eid = jnp.zeros((tm, LANE), jnp.int32)
    rank = jnp.zeros((tm, LANE), jnp.int32)
    gate = jnp.zeros((tm, LANE), F32)
    for k in range(TOP_K):
        rk = jnp.sum(jnp.where(onehot[k], before, 0.0), axis=-1, keepdims=True).astype(jnp.int32)
        eid = jnp.where(lane == k, top_i[k], eid)
        rank = jnp.where(lane == k, rk, rank)
        gate = jnp.where(lane == k, ex[k] / den, gate)
    eid_ref[...] = eid
    rank_ref[...] = rank
    gate_ref[...] = gate
    carry = carry_ref[...] + jnp.sum(cnt, axis=0, keepdims=True)
    carry_ref[...] = carry
    cnt_ref[...] = carry


def _post(x2d, mod, ota, otb, sig, wp, seq, tm):
    n_tok, d = x2d.shape
    n_tiles = n_tok // tm
    tiles_per_seq = seq // tm
    full = lambda a: pl.BlockSpec(a.shape, lambda i: (0,) * a.ndim)
    tok = lambda cols: pl.BlockSpec((tm, cols), lambda i: (i, 0))
    tcol = lambda rows: pl.BlockSpec((rows, tm), lambda i: (0, i))
    in_specs = [tok(d), pl.BlockSpec((None, MOD_ROWS, d), lambda i: (i // tiles_per_seq, 0, 0)),
                tcol(ota.shape[0]), tcol(otb.shape[0]), tok(2 * d),
                full(wp["wbra"]), full(wp["wbrb"]), full(wp["wout"]), full(wp["g_moe"]),
                full(wp["wr"]), full(wp["br"])]
    out_shape = [jax.ShapeDtypeStruct((n_tok, d), F32), jax.ShapeDtypeStruct((n_tok, d // LANE, LANE), F32),
                 jax.ShapeDtypeStruct((n_tok, LANE), jnp.int32), jax.ShapeDtypeStruct((n_tok, LANE), jnp.int32),
                 jax.ShapeDtypeStruct((n_tok, LANE), F32), jax.ShapeDtypeStruct((1, LANE), F32)]
    out_specs = [tok(d), pl.BlockSpec((tm, d // LANE, LANE), lambda i: (i, 0, 0)),
                 tok(LANE), tok(LANE), tok(LANE), pl.BlockSpec((1, LANE), lambda i: (0, 0))]
    return pl.pallas_call(
        _post_kernel, grid=(n_tiles,), in_specs=in_specs, out_specs=out_specs, out_shape=out_shape,
        scratch_shapes=[pltpu.VMEM((1, LANE), F32)],
        compiler_params=_cparams("arbitrary"), name="post",
    )(x2d, mod, ota, otb, sig, wp["wbra"], wp["wbrb"], wp["wout"], wp["g_moe"], wp["wr"], wp["br"])


def _row_copy(src, dst, sem):
    return pltpu.make_async_copy(src, dst, sem)


def _dispatch_kernel(pad_end_ref, dest_ref, u_ref, xs_ref, zero_ref, sem, zero_sem, *, tm):
    @pl.when(pl.program_id(0) == 0)
    def _():
        zero_ref[...] = jnp.zeros_like(zero_ref)

        def tail_copy(e):
            start = pl.multiple_of(pad_end_ref[e] - EXPERT_BLOCK, EXPERT_BLOCK)
            return pltpu.make_async_copy(zero_ref, xs_ref.at[pl.ds(start, EXPERT_BLOCK)], zero_sem)

        def nonempty(e):
            return pad_end_ref[e] > (pad_end_ref[e - 1] if e else 0)

        for e in range(N_EXPERTS):
            pl.when(nonempty(e))(lambda e=e: tail_copy(e).start())
        for e in range(N_EXPERTS):
            pl.when(nonempty(e))(lambda e=e: tail_copy(e).wait())

        def unused_copy(b):
            start = pl.multiple_of(b * EXPERT_BLOCK, EXPERT_BLOCK)
            return pltpu.make_async_copy(zero_ref, xs_ref.at[pl.ds(start, EXPERT_BLOCK)], zero_sem)

        first_unused = pad_end_ref[N_EXPERTS - 1] // EXPERT_BLOCK
        n_blocks = xs_ref.shape[0] // EXPERT_BLOCK
        lax.fori_loop(first_unused, n_blocks, lambda b, c: (unused_copy(b).start(), c)[1], 0)
        lax.fori_loop(first_unused, n_blocks, lambda b, c: (unused_copy(b).wait(), c)[1], 0)

    def issue(r, c):
        for k in range(TOP_K):
            d = dest_ref[0, 0, r * TOP_K + k]
            _row_copy(u_ref.at[pl.ds(r, 1)], xs_ref.at[pl.ds(d, 1)], sem).start(priority=k % 2)
        return c

    lax.fori_loop(0, tm, issue, 0, unroll=ROW_LOOP_UNROLL)

    def drain(r, c):
        for k in range(TOP_K):
            _row_copy(u_ref.at[pl.ds(0, 1)], xs_ref.at[pl.ds(0, 1)], sem).wait()
        return c

    lax.fori_loop(0, tm, drain, 0, unroll=ROW_LOOP_UNROLL)


def _dispatch(u2, dest, pad_end, cap, tm):
    n_tok, row = u2.shape[0], u2.shape[1:]
    n_tiles = n_tok // tm
    dest3 = dest.reshape(n_tiles, 1, tm * TOP_K)
    grid_spec = pltpu.PrefetchScalarGridSpec(
        num_scalar_prefetch=1,
        grid=(n_tiles,),
        in_specs=[pl.BlockSpec((1, 1, tm * TOP_K), lambda i, pe: (i, 0, 0), memory_space=pltpu.SMEM),
                  pl.BlockSpec((tm,) + row, lambda i, pe: (i, 0, 0))],
        out_specs=pl.BlockSpec(memory_space=pl.ANY),
        scratch_shapes=[pltpu.VMEM((EXPERT_BLOCK,) + row, u2.dtype), pltpu.SemaphoreType.DMA(()),
                        pltpu.SemaphoreType.DMA(())],
    )
    return pl.pallas_call(
        functools.partial(_dispatch_kernel, tm=tm), grid_spec=grid_spec,
        out_shape=jax.ShapeDtypeStruct((cap,) + row, u2.dtype),
        compiler_params=_cparams("arbitrary"), name="dispatch",
    )(pad_end, dest3, u2)


def _expert_kernel(be_ref, valid_ref, xs_ref, wgu_ref, bgu_ref, wd_ref, bd_ref, y_ref, wgu_bf, wd_bf):
    b = pl.program_id(0)

    @pl.when(jnp.logical_or(b == 0, be_ref[b] != be_ref[jnp.maximum(b - 1, 0)]))
    def _():
        wgu_bf[...] = wgu_ref[0].astype(BF16)
        wd_bf[...] = wd_ref[0].astype(BF16)

    @pl.when(valid_ref[b] == 1)
    def _():
        x = xs_ref[...].reshape(EXPERT_BLOCK, D_MODEL).astype(BF16)
        h = jnp.dot(x, wgu_bf[...], preferred_element_type=F32) + bgu_ref[0]
        hg = jnp.minimum(h[:, :D_FF], SWIGLU_LIMIT)
        hl = jnp.clip(h[:, D_FF:], -SWIGLU_LIMIT, SWIGLU_LIMIT)
        a = hg * jax.nn.sigmoid(SWIGLU_ALPHA * hg) * (hl + 1.0)
        y = jnp.dot(a.astype(BF16), wd_bf[...], preferred_element_type=F32) + bd_ref[0]
        y_ref[...] = y.reshape(y_ref.shape)

    @pl.when(valid_ref[b] == 0)
    def _():
        y_ref[...] = jnp.zeros_like(y_ref)


def _experts(xs, block_e, block_valid, wp):
    cap, d = xs.shape[0], D_MODEL
    n_blocks = cap // EXPERT_BLOCK
    grid_spec = pltpu.PrefetchScalarGridSpec(
        num_scalar_prefetch=2,
        grid=(n_blocks,),
        in_specs=[pl.BlockSpec((EXPERT_BLOCK,) + xs.shape[1:], lambda b, be, bv: (jnp.where(bv[b] == 1, b, 0), 0, 0)),
                  pl.BlockSpec((1, d, 2 * D_FF), lambda b, be, bv: (be[b], 0, 0)),
                  pl.BlockSpec((1, 1, 2 * D_FF), lambda b, be, bv: (be[b], 0, 0)),
                  pl.BlockSpec((1, D_FF, d), lambda b, be, bv: (be[b], 0, 0)),
                  pl.BlockSpec((1, 1, d), lambda b, be, bv: (be[b], 0, 0))],
        out_specs=pl.BlockSpec((EXPERT_BLOCK,) + xs.shape[1:], lambda b, be, bv: (b, 0, 0)),
        scratch_shapes=[pltpu.VMEM((d, 2 * D_FF), BF16), pltpu.VMEM((D_FF, d), BF16)],
    )
    return pl.pallas_call(
        _expert_kernel, grid_spec=grid_spec,
        out_shape=jax.ShapeDtypeStruct(xs.shape, F32),
        compiler_params=_cparams("arbitrary"), name="experts",
    )(block_e, block_valid, xs, wp["wgu"], wp["bgu"], wp["wd"], wp["bd"])


def _combine_kernel(dest_ref, dest_next_ref, x1_ref, gate_ref, mod_ref, gfin_ref, y_hbm, o_ref, buf, sem, *, tm):
    i = pl.program_id(0)
    slot = i % 2

    def gather(dref, s):
        def issue(r, c):
            for k in range(TOP_K):
                d = dref[0, 0, r * TOP_K + k]
                _row_copy(y_hbm.at[pl.ds(d, 1)], buf.at[s, k, pl.ds(r, 1)], sem.at[s]).start(priority=k % 2)
            return c

        lax.fori_loop(0, tm, issue, 0, unroll=ROW_LOOP_UNROLL)

    @pl.when(i == 0)
    def _():
        gather(dest_ref, slot)

    @pl.when(i + 1 < pl.num_programs(0))
    def _():
        gather(dest_next_ref, 1 - slot)

    def drain(r, c):
        for k in range(TOP_K):
            _row_copy(y_hbm.at[pl.ds(0, 1)], buf.at[slot, k, pl.ds(0, 1)], sem.at[slot]).wait()
        return c

    lax.fori_loop(0, tm, drain, 0, unroll=ROW_LOOP_UNROLL)
    gate = gate_ref[...]
    rows = lambda k: buf[slot, k].reshape(tm, D_MODEL)
    y = gate[:, 0:1] * rows(0)
    for k in range(1, TOP_K):
        y = y + gate[:, k:k + 1] * rows(k)
    x2 = x1_ref[...] + mod_ref[_GT_M:_GT_M + 1, :] * y
    o_ref[...] = _rms(x2, gfin_ref[...])


def _combine(x1, gate, mod, g_final, yb, dest, seq, tm):
    n_tok, d = x1.shape
    n_tiles = n_tok // tm
    tiles_per_seq = seq // tm
    dest3 = dest.reshape(n_tiles, 1, tm * TOP_K)
    return pl.pallas_call(
        functools.partial(_combine_kernel, tm=tm),
        grid=(n_tiles,),
        in_specs=[pl.BlockSpec((1, 1, tm * TOP_K), lambda i: (i, 0, 0), memory_space=pltpu.SMEM),
                  pl.BlockSpec((1, 1, tm * TOP_K), lambda i: (jnp.minimum(i + 1, n_tiles - 1), 0, 0),
                               memory_space=pltpu.SMEM),
                  pl.BlockSpec((tm, d), lambda i: (i, 0)),
                  pl.BlockSpec((tm, LANE), lambda i: (i, 0)),
                  pl.BlockSpec((None, MOD_ROWS, d), lambda i: (i // tiles_per_seq, 0, 0)),
                  pl.BlockSpec((1, d), lambda i: (0, 0)),
                  pl.BlockSpec(memory_space=pl.ANY)],
        out_specs=pl.BlockSpec((tm, d), lambda i: (i, 0)),
        out_shape=jax.ShapeDtypeStruct((n_tok, d), F32),
        scratch_shapes=[pltpu.VMEM((2, TOP_K, tm) + yb.shape[1:], F32), pltpu.SemaphoreType.DMA((2,))],
        compiler_params=_cparams("arbitrary"), name="combine",
    )(dest3, dest3, x1, gate, mod, g_final, yb)


def _slot_cols(w, n_heads, width, offset=0):
    k = w.shape[0]
    w3 = w.reshape(k, n_heads, width)
    out = jnp.pad(w3, ((0, 0), (0, 0), (offset, LANE - offset - width)))
    return out.reshape(k, n_heads * LANE)


def _prep_weights(g_attn, g_moe, w_in, g_qn, g_kn, g_qlat, w_uq, g_kvlat, w_ukv, w_br_a, w_br_b, w_out,
                  w_router, b_router, w_gu, b_gu, w_down, b_down):
    d = D_MODEL
    wa = N_HEADS_A * HEAD_DIM_A
    wk = N_KV_A * HEAD_DIM_A
    c0 = 0
    w_qa = w_in[:, c0:c0 + wa]; c0 += wa
    w_ka = w_in[:, c0:c0 + wk]; c0 += wk
    w_va = w_in[:, c0:c0 + wk]; c0 += wk
    w_ql = w_in[:, c0:c0 + Q_LORA]; c0 += Q_LORA
    w_kvl = w_in[:, c0:c0 + KV_LORA]; c0 += KV_LORA
    w_kpe = w_in[:, c0:c0 + ROPE_DIM_B]; c0 += ROPE_DIM_B
    w_gate = w_in[:, c0:]
    w1 = jnp.concatenate([
        _slot_cols(w_qa, N_HEADS_A, HEAD_DIM_A), _slot_cols(w_ka, N_KV_A, HEAD_DIM_A), w_ql, w_kvl,
        _slot_cols(w_kpe, 1, ROPE_DIM_B, NOPE_DIM), w_gate], axis=1).astype(BF16)
    ukv = w_ukv.reshape(KV_LORA, N_HEADS_B, NOPE_DIM + V_DIM_B)
    w_uk = ukv[:, :, :NOPE_DIM].reshape(KV_LORA, N_HEADS_B * NOPE_DIM)
    w_uv = ukv[:, :, NOPE_DIM:].reshape(KV_LORA, N_HEADS_B * V_DIM_B)
    pad_gain = lambda g: jnp.zeros((1, LANE), F32).at[0, :g.shape[0]].set(g)
    return {
        "g_attn": g_attn.reshape(1, d), "g_moe": g_moe.reshape(1, d),
        "w1": w1, "wvat": w_va.T.astype(BF16),
        "g_qa": pad_gain(g_qn), "g_ka": pad_gain(g_kn),
        "g_qlat": g_qlat.reshape(1, Q_LORA), "g_kvlat": g_kvlat.reshape(1, KV_LORA),
        "wuq": _slot_cols(w_uq, N_HEADS_B, NOPE_DIM + ROPE_DIM_B).astype(BF16),
        "wuk": _slot_cols(w_uk, N_HEADS_B, NOPE_DIM).astype(BF16),
        "wuvt": w_uv.T.astype(BF16),
        "wbra": w_br_a.astype(BF16), "wbrb": w_br_b.astype(BF16), "wout": w_out.astype(BF16),
        "wr": jnp.zeros((d, LANE), F32).at[:, :N_EXPERTS].set(w_router).astype(BF16),
        "br": jnp.full((1, LANE), NEG_BIG, F32).at[0, :N_EXPERTS].set(b_router),
        "wgu": w_gu, "bgu": b_gu.reshape(N_EXPERTS, 1, 2 * D_FF),
        "wd": w_down, "bd": b_down.reshape(N_EXPERTS, 1, d),
    }


def _rope_parts(n_rows):
    def three(rot_dim, offset, pos, second_half):
        half = rot_dim // 2
        quarter = half // 2
        inv = ROPE_THETA ** (-np.arange(0, half, 2, dtype=np.float64) / half)
        ang = np.asarray(pos, np.float64)[:, None] * inv
        lo = offset + (half if second_half else 0)
        cos, sinm, sinp = (np.zeros((len(pos), LANE)) for _ in range(3))
        cos[:, lo:lo + quarter] = cos[:, lo + quarter:lo + half] = np.cos(ang)
        sinm[:, lo:lo + quarter] = -np.sin(ang)
        sinp[:, lo + quarter:lo + half] = np.sin(ang)
        if second_half:
            cos[:, :offset] = 1.0
            cos[:, offset + rot_dim:] = 1.0
        return [cos, sinm, sinp]

    def part(pos, second_half):
        tabs = three(HEAD_DIM_A, 0, pos, second_half) + three(ROPE_DIM_B, NOPE_DIM, pos, second_half)
        return jnp.asarray(np.concatenate(tabs, axis=1), F32)

    return part(np.arange(n_rows), False), part(np.arange(GRID_W), True)


def _trunk(x, mod, wp, g_final, rope_row, rope_col):
    n_batch, seq, d = x.shape
    n_tok = n_batch * seq
    x2d = x.reshape(n_tok, d)
    tm = min(256, seq)

    qa, ka, vta, qb, kb, vtb, sig = _pre(x2d, mod, wp, rope_row, rope_col, seq, tm)
    cols = ATTN_COLS
    ota = _attention(qa, ka, vta, n_batch=n_batch, seq=seq, n_groups=N_KV_A, n_sub=GROUP_A,
                     tq=min(cols // GROUP_A, seq), dv=HEAD_DIM_A)
    otb = _attention(qb, kb, vtb, n_batch=n_batch, seq=seq, n_groups=N_HEADS_B, n_sub=1,
                     tq=min(cols, seq), dv=V_DIM_B)
    x1, u2, eid, rank, gate, cnt = _post(x2d, mod, ota, otb, sig, wp, seq, tm)

    n_assign = n_tok * TOP_K
    n_blocks = -(-n_assign // EXPERT_BLOCK) + N_EXPERTS
    cap = n_blocks * EXPERT_BLOCK
    counts = cnt[0, :N_EXPERTS].astype(jnp.int32)
    padded = (counts + EXPERT_BLOCK - 1) // EXPERT_BLOCK * EXPERT_BLOCK
    pad_end = jnp.cumsum(padded)
    pad_start = pad_end - padded
    experts = jnp.arange(N_EXPERTS, dtype=jnp.int32)
    start = jnp.sum(jnp.where(eid[:, :TOP_K, None] == experts, pad_start.astype(jnp.int32), 0), axis=-1)
    dest = start + rank[:, :TOP_K]
    block_lo = jnp.arange(n_blocks, dtype=jnp.int32) * EXPERT_BLOCK
    block_e = jnp.minimum(jnp.sum(block_lo[:, None] >= pad_end[None, :], axis=1), N_EXPERTS - 1).astype(jnp.int32)
    block_valid = (block_lo < pad_end[-1]).astype(jnp.int32)

    xs = _dispatch(u2, dest, pad_end.astype(jnp.int32), cap, min(DISPATCH_TILE, n_tok))
    yb = _experts(xs, block_e, block_valid, wp)
    out = _combine(x1, gate, mod, g_final.reshape(1, d), yb, dest, seq, tm)
    return out.reshape(n_batch, seq, d)


def kernel(x_prompt, x_sample, c_prompt, c_sample, w_mod, b_mod, g_attn, g_moe, w_in, g_qn, g_kn, g_qlat,
           w_uq, g_kvlat, w_ukv, w_br_a, w_br_b, w_out, w_router, b_router, w_gu, b_gu, w_down, b_down,
           g_final):
    assert w_mod.shape[0] == 1, "single-layer trunk"
    wp = _prep_weights(g_attn[0], g_moe[0], w_in[0], g_qn[0], g_kn[0], g_qlat[0], w_uq[0], g_kvlat[0],
                       w_ukv[0], w_br_a[0], w_br_b[0], w_out[0], w_router[0], b_router[0], w_gu[0],
                       b_gu[0], w_down[0], b_down[0])
    nb_p = c_prompt.shape[0]
    mod = _modulation(jnp.concatenate([c_prompt, c_sample], axis=0), w_mod[0], b_mod[0])
    max_seq = max(x_prompt.shape[1], x_sample.shape[1])
    rope_row, rope_col = _rope_parts(max_seq // GRID_W)
    y_prompt = _trunk(x_prompt, mod[:nb_p], wp, g_final, rope_row, rope_col)
    y_sample = _trunk(x_sample, mod[nb_p:], wp, g_final, rope_row, rope_col)
    return (y_prompt, y_sample)
```

```python
import functools

import numpy as np
import jax
import jax.numpy as jnp
from jax import lax
from jax.experimental import pallas as pl
from jax.experimental.pallas import tpu as pltpu

F32 = jnp.float32
BF16 = jnp.bfloat16

D_MODEL = 1024
GRID_W = 64
ROPE_THETA = 10000.0
EPS = 1e-6
N_HEADS_A = 8
N_KV_A = 2
GROUP_A = N_HEADS_A // N_KV_A
HEAD_DIM_A = 64
N_HEADS_B = 8
Q_LORA = 768
KV_LORA = 256
NOPE_DIM = 64
ROPE_DIM_B = 32
V_DIM_B = 64
N_EXPERTS = 32
TOP_K = 4
D_FF = D_MODEL
SWIGLU_ALPHA = 1.702
SWIGLU_LIMIT = 7.0
EXPERT_BLOCK = 256

LANE = 128
BF16_SUBLANES = 16
NEG_BIG = -1e30
LOG2E = 1.4426950408889634
ONES_ROWS = BF16_SUBLANES
VMEM_LIMIT = 52 * 1024 * 1024
ATTN_LAG = 2
ATTN_RING = 4
ATTN_COLS = 1024
ATTN_UNROLL = 32
DISPATCH_TILE = 512
ROW_LOOP_UNROLL = 4
BASE_LANE = LANE - 1
DEN_MIN, DEN_MAX = 1e-30, 1e30

_C_QA = 0
_C_KA = _C_QA + N_HEADS_A * LANE
_C_QLAT = _C_KA + N_KV_A * LANE
_C_KVLAT = _C_QLAT + Q_LORA
_C_KPE = _C_KVLAT + KV_LORA
_C_GATE = _C_KPE + LANE
_C_END = _C_GATE + 2 * D_MODEL

_SH_A, _SC_A, _GT_A, _SH_M, _SC_M, _GT_M = range(6)
MOD_ROWS = 8

_NT = (((1,), (1,)), ((), ()))
_TN = (((0,), (0,)), ((), ()))


def _cparams(*sem):
    return pltpu.CompilerParams(dimension_semantics=sem, vmem_limit_bytes=VMEM_LIMIT)


def _rms(x, g):
    return x * lax.rsqrt(jnp.mean(x * x, axis=-1, keepdims=True) + EPS) * g


def _mod_kernel(c_ref, w_ref, b_ref, o_ref):
    c = c_ref[...]
    s = c * jax.nn.sigmoid(c)
    o_ref[...] = jnp.dot(s.astype(BF16), w_ref[...].astype(BF16), preferred_element_type=F32) + b_ref[...]


def _modulation(c, w_mod, b_mod):
    nb, d = c.shape
    rows = -(-nb // 8) * 8
    cp = jnp.zeros((rows, d), F32).at[:nb].set(c)
    n_out = w_mod.shape[1]
    tn = 512
    out = pl.pallas_call(
        _mod_kernel,
        grid=(n_out // tn,),
        in_specs=[pl.BlockSpec((rows, d), lambda j: (0, 0)),
                  pl.BlockSpec((d, tn), lambda j: (0, j)),
                  pl.BlockSpec((1, tn), lambda j: (0, j))],
        out_specs=pl.BlockSpec((rows, tn), lambda j: (0, j)),
        out_shape=jax.ShapeDtypeStruct((rows, n_out), F32),
        compiler_params=_cparams("arbitrary"),
        name="mod",
    )(cp, w_mod, b_mod.reshape(1, n_out))
    mod = out[:nb].reshape(nb, 6, d)
    return jnp.concatenate([mod, jnp.zeros((nb, MOD_ROWS - 6, d), F32)], axis=1)


def _rope(x, cos, sinm, sinp, shift):
    return x * cos + pltpu.roll(x, LANE - shift, 1) * sinm + pltpu.roll(x, shift, 1) * sinp


def _pre_kernel(x_ref, mod_ref, gattn_ref, w1_ref, wvat_ref, gqa_ref, gka_ref, gqlat_ref, wuq_ref,
                gkvlat_ref, wuk_ref, wuvt_ref, rrow_ref, rcol_ref,
                qa_ref, ka_ref, vta_ref, qb_ref, kb_ref, vtb_ref, sig_ref):
    x = x_ref[...]
    u = _rms(x, gattn_ref[...]) * (1.0 + mod_ref[_SC_A:_SC_A + 1, :]) + mod_ref[_SH_A:_SH_A + 1, :]
    ub = u.astype(BF16)

    def proj(lo, hi):
        return jnp.dot(ub, w1_ref[:, lo:hi], preferred_element_type=F32)

    rcol = rcol_ref[...]
    tab = jnp.concatenate([rrow_ref[r:r + 1, :] + rcol for r in range(rrow_ref.shape[0])], axis=0)
    cos_a, sinm_a, sinp_a, cos_b, sinm_b, sinp_b = (tab[:, j * LANE:(j + 1) * LANE] for j in range(6))
    shift_a = HEAD_DIM_A // 4
    shift_b = ROPE_DIM_B // 4

    def head_a(xs, g, scale):
        ms = jnp.sum(xs * xs, axis=-1, keepdims=True) * (1.0 / HEAD_DIM_A)
        xn = xs * lax.rsqrt(ms + EPS) * g
        return (_rope(xn, cos_a, sinm_a, sinp_a, shift_a) * scale).astype(BF16)

    qa = proj(_C_QA, _C_KA)
    for h in range(N_HEADS_A):
        qa_ref[:, h * LANE:(h + 1) * LANE] = head_a(qa[:, h * LANE:(h + 1) * LANE], gqa_ref[...],
                                                    HEAD_DIM_A ** -0.5 * LOG2E)
    ka = proj(_C_KA, _C_QLAT)
    base_lane = lax.broadcasted_iota(jnp.int32, (x.shape[0], LANE), 1) == BASE_LANE
    for h in range(N_KV_A):
        kh = head_a(ka[:, h * LANE:(h + 1) * LANE], gka_ref[...], 1.0)
        ka_ref[:, h * LANE:(h + 1) * LANE] = jnp.where(base_lane, 1.0, kh).astype(BF16)
    vta = lax.dot_general(wvat_ref[...], ub, _NT, preferred_element_type=F32)
    vta_ref[0, :, :HEAD_DIM_A, :] = vta.astype(BF16).reshape(N_KV_A, HEAD_DIM_A, vta.shape[-1])
    vta_ref[0, :, HEAD_DIM_A:, :] = jnp.ones((N_KV_A, ONES_ROWS, vta.shape[-1]), BF16)

    qn = _rms(proj(_C_QLAT, _C_KVLAT), gqlat_ref[...]).astype(BF16)
    qb = jnp.dot(qn, wuq_ref[...], preferred_element_type=F32)
    scale_b = (NOPE_DIM + ROPE_DIM_B) ** -0.5 * LOG2E
    for h in range(N_HEADS_B):
        sl = slice(h * LANE, (h + 1) * LANE)
        qb_ref[:, sl] = (_rope(qb[:, sl], cos_b, sinm_b, sinp_b, shift_b) * scale_b).astype(BF16)
    kvn = _rms(proj(_C_KVLAT, _C_KPE), gkvlat_ref[...]).astype(BF16)
    kpe = _rope(proj(_C_KPE, _C_GATE), cos_b, sinm_b, sinp_b, shift_b)
    kb = jnp.dot(kvn, wuk_ref[...], preferred_element_type=F32)
    for h in range(N_HEADS_B):
        sl = slice(h * LANE, (h + 1) * LANE)
        kb_ref[:, sl] = jnp.where(base_lane, 1.0, kb[:, sl] + kpe).astype(BF16)
    vtb = lax.dot_general(wuvt_ref[...], kvn, _NT, preferred_element_type=F32)
    vtb_ref[0, :, :V_DIM_B, :] = vtb.astype(BF16).reshape(N_HEADS_B, V_DIM_B, vtb.shape[-1])
    vtb_ref[0, :, V_DIM_B:, :] = jnp.ones((N_HEADS_B, ONES_ROWS, vtb.shape[-1]), BF16)

    sig_ref[...] = jax.nn.sigmoid(proj(_C_GATE, _C_END)).astype(BF16)


def _pre(x2d, mod, wp, rope_row, rope_col, seq, tm):
    n_tok, d = x2d.shape
    n_tiles = n_tok // tm
    tiles_per_seq = seq // tm
    assert tm % GRID_W == 0, "a token tile covers whole grid rows"
    rows_per_tile = tm // GRID_W
    rope_row = rope_row[:seq // GRID_W].reshape(tiles_per_seq, rows_per_tile, rope_row.shape[-1])
    full = lambda a: pl.BlockSpec(a.shape, lambda i: (0,) * a.ndim)
    tok = lambda cols: pl.BlockSpec((tm, cols), lambda i: (i, 0))
    in_specs = [tok(d),
                pl.BlockSpec((None, MOD_ROWS, d), lambda i: (i // tiles_per_seq, 0, 0)),
                full(wp["g_attn"]), full(wp["w1"]), full(wp["wvat"]), full(wp["g_qa"]), full(wp["g_ka"]),
                full(wp["g_qlat"]), full(wp["wuq"]), full(wp["g_kvlat"]), full(wp["wuk"]), full(wp["wuvt"]),
                pl.BlockSpec((None, rows_per_tile, rope_row.shape[-1]), lambda i: (i % tiles_per_seq, 0, 0)),
                full(rope_col)]
    out_shape = [
        jax.ShapeDtypeStruct((n_tok, N_HEADS_A * LANE), BF16),
        jax.ShapeDtypeStruct((n_tok, N_KV_A * LANE), BF16),
        jax.ShapeDtypeStruct((n_tiles, N_KV_A, HEAD_DIM_A + ONES_ROWS, tm), BF16),
        jax.ShapeDtypeStruct((n_tok, N_HEADS_B * LANE), BF16),
        jax.ShapeDtypeStruct((n_tok, N_HEADS_B * LANE), BF16),
        jax.ShapeDtypeStruct((n_tiles, N_HEADS_B, V_DIM_B + ONES_ROWS, tm), BF16),
        jax.ShapeDtypeStruct((n_tok, 2 * D_MODEL), BF16),
    ]
    out_specs = [tok(N_HEADS_A * LANE), tok(N_KV_A * LANE),
                 pl.BlockSpec((1, N_KV_A, HEAD_DIM_A + ONES_ROWS, tm), lambda i: (i, 0, 0, 0)),
                 tok(N_HEADS_B * LANE), tok(N_HEADS_B * LANE),
                 pl.BlockSpec((1, N_HEADS_B, V_DIM_B + ONES_ROWS, tm), lambda i: (i, 0, 0, 0)),
                 tok(2 * D_MODEL)]
    return pl.pallas_call(
        _pre_kernel, grid=(n_tiles,), in_specs=in_specs, out_specs=out_specs, out_shape=out_shape,
        compiler_params=_cparams("arbitrary"), name="pre",
    )(x2d, mod, wp["g_attn"], wp["w1"], wp["wvat"], wp["g_qa"], wp["g_ka"], wp["g_qlat"], wp["wuq"],
      wp["g_kvlat"], wp["wuk"], wp["wuvt"], rope_row, rope_col)


def _attn_kernel(q_ref, k_ref, vt_ref, o_ref, kmax_ref, *p_bufs, n_sub, tq, tk, n_chunks, dv, unroll):
    lag = ATTN_LAG
    cols = n_sub * tq
    if n_sub == 1:
        q = q_ref[...]
    else:
        q = jnp.concatenate([q_ref[:, j * LANE:(j + 1) * LANE] for j in range(n_sub)], axis=0)

    def key_chunk(i):
        start = i * tk if isinstance(i, int) else pl.multiple_of(i * tk, tk)
        return k_ref[pl.ds(start, tk), :]

    @pl.when(pl.program_id(2) == 0)
    def _():
        def norm_body(c, mx):
            kc = key_chunk(c).astype(F32)
            n2 = jnp.sum(kc * kc, axis=-1, keepdims=True)
            return jnp.maximum(mx, jnp.max(n2, axis=0, keepdims=True))

        kmax2 = lax.fori_loop(0, n_chunks, norm_body, jnp.zeros((1, 1), F32))
        kmax_ref[...] = jnp.broadcast_to(jnp.sqrt(jnp.maximum(kmax2 - 1.0, 0.0)), kmax_ref.shape)

    qt = q.astype(F32).T
    base = jnp.sqrt(jnp.sum(qt * qt, axis=0, keepdims=True)) * kmax_ref[0:1, 0:1]
    row = lax.broadcasted_iota(jnp.int32, qt.shape, 0)
    qt_aug = jnp.where(row == BASE_LANE, -base, qt).astype(BF16)

    def pv(c, p_ref):
        return jnp.dot(vt_ref[c], p_ref[...], preferred_element_type=F32)

    def trip(i, u, acc):
        if not isinstance(i, int) or i >= lag:
            acc = acc + pv(i - lag, p_bufs[(u - lag) % ATTN_RING])
        s = jnp.dot(key_chunk(i), qt_aug, preferred_element_type=F32)
        p_bufs[u % ATTN_RING][...] = jnp.exp2(s).astype(BF16)
        return acc

    def body(j, acc):
        for u in range(unroll):
            acc = trip(j * unroll + u, u, acc)
        return acc

    acc = jnp.zeros((dv + ONES_ROWS, cols), F32)
    for u in range(unroll):
        acc = trip(u, u, acc)
    acc = lax.fori_loop(1, n_chunks // unroll, body, acc)
    for c in range(n_chunks - lag, n_chunks):
        acc = acc + pv(c, p_bufs[c % ATTN_RING])

    def emit(acc):
        o = acc[:dv] / acc[dv:dv + 1]
        for j in range(n_sub):
            o_ref[j * dv:(j + 1) * dv, :] = o[:, j * tq:(j + 1) * tq].astype(BF16)

    emit(acc)
    den = acc[dv:dv + 1]
    safe = jnp.logical_and(den > DEN_MIN, den < DEN_MAX)
    n_unsafe = jnp.sum(jnp.where(safe, 0.0, 1.0))

    @pl.when(n_unsafe > 0.0)
    def _():
        def exact_body(i, carry):
            m, acc = carry
            s = lax.dot_general(key_chunk(i), q, _NT, preferred_element_type=F32)
            m_new = jnp.maximum(m, jnp.max(s, axis=0, keepdims=True))
            p = jnp.exp2(s - m_new).astype(BF16)
            return m_new, jnp.exp2(m - m_new) * acc + jnp.dot(vt_ref[i], p, preferred_element_type=F32)

        init = (jnp.full((1, cols), NEG_BIG, F32), jnp.zeros((dv + ONES_ROWS, cols), F32))
        emit(lax.fori_loop(0, n_chunks, exact_body, init)[1])


def _attention(q, k, vt, *, n_batch, seq, n_groups, n_sub, tq, dv):
    n_tok = q.shape[0]
    tk = vt.shape[-1]
    n_chunks = seq // tk
    q_tiles = seq // tq
    unroll = ATTN_UNROLL
    while n_chunks % unroll:
        unroll //= 2
    assert unroll % ATTN_RING == 0, "buffer slots must be static per trip"
    kern = functools.partial(_attn_kernel, n_sub=n_sub, tq=tq, tk=tk, n_chunks=n_chunks, dv=dv,
                             unroll=unroll)
    cols = n_sub * tq
    return pl.pallas_call(
        kern,
        grid=(n_batch, n_groups, q_tiles),
        in_specs=[pl.BlockSpec((tq, n_sub * LANE), lambda b, g, i: (b * q_tiles + i, g)),
                  pl.BlockSpec((seq, LANE), lambda b, g, i: (b, g)),
                  pl.BlockSpec((n_chunks, None, dv + ONES_ROWS, tk), lambda b, g, i: (b, g, 0, 0))],
        out_specs=pl.BlockSpec((n_sub * dv, tq), lambda b, g, i: (g, b * q_tiles + i)),
        out_shape=jax.ShapeDtypeStruct((n_groups * n_sub * dv, n_tok), BF16),
        scratch_shapes=[pltpu.VMEM((8, LANE), F32)] + [pltpu.VMEM((tk, cols), BF16)] * ATTN_RING,
        compiler_params=_cparams("arbitrary", "arbitrary", "arbitrary"),
        name="attn",
    )(q, k, vt)


def _post_kernel(x_ref, mod_ref, ota_ref, otb_ref, sig_ref, wbra_ref, wbrb_ref, wout_ref, gmoe_ref,
                 wr_ref, br_ref,
                 x1_ref, u2_ref, eid_ref, rank_ref, gate_ref, cnt_ref, carry_ref):
    i = pl.program_id(0)

    @pl.when(i == 0)
    def _():
        carry_ref[...] = jnp.zeros_like(carry_ref)

    d = D_MODEL
    ya = lax.dot_general(ota_ref[...], wbra_ref[...], _TN, preferred_element_type=F32)
    yb = lax.dot_general(otb_ref[...], wbrb_ref[...], _TN, preferred_element_type=F32)
    mixed = sig_ref[:, :d].astype(F32) * ya + sig_ref[:, d:].astype(F32) * yb
    att = jnp.dot(mixed.astype(BF16), wout_ref[...], preferred_element_type=F32)
    x1 = x_ref[...] + mod_ref[_GT_A:_GT_A + 1, :] * att
    x1_ref[...] = x1
    u2 = _rms(x1, gmoe_ref[...]) * (1.0 + mod_ref[_SC_M:_SC_M + 1, :]) + mod_ref[_SH_M:_SH_M + 1, :]
    u2_ref[...] = u2.reshape(u2_ref.shape)

    logits = jnp.dot(u2.astype(BF16), wr_ref[...], preferred_element_type=F32) + br_ref[...]
    tm = logits.shape[0]
    lane = lax.broadcasted_iota(jnp.int32, (tm, LANE), 1)
    vals = logits
    top_v, top_i = [], []
    for _ in range(TOP_K):
        mx = jnp.max(vals, axis=-1, keepdims=True)
        idx = jnp.min(jnp.where(vals == mx, lane, LANE), axis=-1, keepdims=True)
        top_v.append(mx)
        top_i.append(idx)
        vals = jnp.where(lane == idx, -jnp.inf, vals)
    ex = [jnp.exp(v - top_v[0]) for v in top_v]
    den = ex[0] + ex[1] + ex[2] + ex[3]

    onehot = [lane == idx for idx in top_i]
    cnt = sum(oh.astype(F32) for oh in onehot)
    r_i = lax.broadcasted_iota(jnp.int32, (tm, tm), 0)
    c_i = lax.broadcasted_iota(jnp.int32, (tm, tm), 1)
    lower = (c_i < r_i).astype(BF16)
    before = jnp.dot(lower, cnt.astype(BF16), preferred_element_type=F32) + carry_ref[...]
    eid = jnp.zeros((tm, LANE), jnp.int32)
    rank = jnp.zeros((tm, LANE), jnp.int32)
    gate = jnp.zeros((tm, LANE), F32)
    for k in range(TOP_K):
        rk = jnp.sum(jnp.where(onehot[k], before, 0.0), axis=-1, keepdims=True).astype(jnp.int32)
        eid = jnp.where(lane == k, top_i[k], eid)
        rank = jnp.where(lane == k, rk, rank)
        gate = jnp.where(lane == k, ex[k] / den, gate)
    eid_ref[...] = eid
    rank_ref[...] = rank
    gate_ref[...] = gate
    carry = carry_ref[...] + jnp.sum(cnt, axis=0, keepdims=True)
    carry_ref[...] = carry
    cnt_ref[...] = carry


def _post(x2d, mod, ota, otb, sig, wp, seq, tm):
    n_tok, d = x2d.shape
    n_tiles = n_tok // tm
    tiles_per_seq = seq // tm
    full = lambda a: pl.BlockSpec(a.shape, lambda i: (0,) * a.ndim)
    tok = lambda cols: pl.BlockSpec((tm, cols), lambda i: (i, 0))
    tcol = lambda rows: pl.BlockSpec((rows, tm), lambda i: (0, i))
    in_specs = [tok(d), pl.BlockSpec((None, MOD_ROWS, d), lambda i: (i // tiles_per_seq, 0, 0)),
                tcol(ota.shape[0]), tcol(otb.shape[0]), tok(2 * d),
                full(wp["wbra"]), full(wp["wbrb"]), full(wp["wout"]), full(wp["g_moe"]),
                full(wp["wr"]), full(wp["br"])]
    out_shape = [jax.ShapeDtypeStruct((n_tok, d), F32), jax.ShapeDtypeStruct((n_tok, d // LANE, LANE), F32),
                 jax.ShapeDtypeStruct((n_tok, LANE), jnp.int32), jax.ShapeDtypeStruct((n_tok, LANE), jnp.int32),
                 jax.ShapeDtypeStruct((n_tok, LANE), F32), jax.ShapeDtypeStruct((1, LANE), F32)]
    out_specs = [tok(d), pl.BlockSpec((tm, d // LANE, LANE), lambda i: (i, 0, 0)),
                 tok(LANE), tok(LANE), tok(LANE), pl.BlockSpec((1, LANE), lambda i: (0, 0))]
    return pl.pallas_call(
        _post_kernel, grid=(n_tiles,), in_specs=in_specs, out_specs=out_specs, out_shape=out_shape,
        scratch_shapes=[pltpu.VMEM((1, LANE), F32)],
        compiler_params=_cparams("arbitrary"), name="post",
    )(x2d, mod, ota, otb, sig, wp["wbra"], wp["wbrb"], wp["wout"], wp["g_moe"], wp["wr"], wp["br"])


def _row_copy(src, dst, sem):
    return pltpu.make_async_copy(src, dst, sem)


def _dispatch_kernel(pad_end_ref, dest_ref, u_ref, xs_ref, zero_ref, sem, zero_sem, *, tm):
    @pl.when(pl.program_id(0) == 0)
    def _():
        zero_ref[...] = jnp.zeros_like(zero_ref)

        def tail_copy(e):
            start = pl.multiple_of(pad_end_ref[e] - EXPERT_BLOCK, EXPERT_BLOCK)
            return pltpu.make_async_copy(zero_ref, xs_ref.at[pl.ds(start, EXPERT_BLOCK)], zero_sem)

        def nonempty(e):
            return pad_end_ref[e] > (pad_end_ref[e - 1] if e else 0)

        for e in range(N_EXPERTS):
            pl.when(nonempty(e))(lambda e=e: tail_copy(e).start())
        for e in range(N_EXPERTS):
            pl.when(nonempty(e))(lambda e=e: tail_copy(e).wait())

        def unused_copy(b):
            start = pl.multiple_of(b * EXPERT_BLOCK, EXPERT_BLOCK)
            return pltpu.make_async_copy(zero_ref, xs_ref.at[pl.ds(start, EXPERT_BLOCK)], zero_sem)

        first_unused = pad_end_ref[N_EXPERTS - 1] // EXPERT_BLOCK
        n_blocks = xs_ref.shape[0] // EXPERT_BLOCK
        lax.fori_loop(first_unused, n_blocks, lambda b, c: (unused_copy(b).start(), c)[1], 0)
        lax.fori_loop(first_unused, n_blocks, lambda b, c: (unused_copy(b).wait(), c)[1], 0)

    def issue(r, c):
        for k in range(TOP_K):
            d = dest_ref[0, 0, r * TOP_K + k]
            _row_copy(u_ref.at[pl.ds(r, 1)], xs_ref.at[pl.ds(d, 1)], sem).start(priority=k % 2)
        return c

    lax.fori_loop(0, tm, issue, 0, unroll=ROW_LOOP_UNROLL)

    def drain(r, c):
        for k in range(TOP_K):
            _row_copy(u_ref.at[pl.ds(0, 1)], xs_ref.at[pl.ds(0, 1)], sem).wait()
        return c

    lax.fori_loop(0, tm, drain, 0, unroll=ROW_LOOP_UNROLL)


def _dispatch(u2, dest, pad_end, cap, tm):
    n_tok, row = u2.shape[0], u2.shape[1:]
    n_tiles = n_tok // tm
    dest3 = dest.reshape(n_tiles, 1, tm * TOP_K)
    grid_spec = pltpu.PrefetchScalarGridSpec(
        num_scalar_prefetch=1,
        grid=(n_tiles,),
        in_specs=[pl.BlockSpec((1, 1, tm * TOP_K), lambda i, pe: (i, 0, 0), memory_space=pltpu.SMEM),
                  pl.BlockSpec((tm,) + row, lambda i, pe: (i, 0, 0))],
        out_specs=pl.BlockSpec(memory_space=pl.ANY),
        scratch_shapes=[pltpu.VMEM((EXPERT_BLOCK,) + row, u2.dtype), pltpu.SemaphoreType.DMA(()),
                        pltpu.SemaphoreType.DMA(())],
    )
    return pl.pallas_call(
        functools.partial(_dispatch_kernel, tm=tm), grid_spec=grid_spec,
        out_shape=jax.ShapeDtypeStruct((cap,) + row, u2.dtype),
        compiler_params=_cparams("arbitrary"), name="dispatch",
    )(pad_end, dest3, u2)


def _expert_kernel(be_ref, valid_ref, xs_ref, wgu_ref, bgu_ref, wd_ref, bd_ref, y_ref, wgu_bf, wd_bf):
    b = pl.program_id(0)

    @pl.when(jnp.logical_or(b == 0, be_ref[b] != be_ref[jnp.maximum(b - 1, 0)]))
    def _():
        wgu_bf[...] = wgu_ref[0].astype(BF16)
        wd_bf[...] = wd_ref[0].astype(BF16)

    @pl.when(valid_ref[b] == 1)
    def _():
        x = xs_ref[...].reshape(EXPERT_BLOCK, D_MODEL).astype(BF16)
        h = jnp.dot(x, wgu_bf[...], preferred_element_type=F32) + bgu_ref[0]
        hg = jnp.minimum(h[:, :D_FF], SWIGLU_LIMIT)
        hl = jnp.clip(h[:, D_FF:], -SWIGLU_LIMIT, SWIGLU_LIMIT)
        a = hg * jax.nn.sigmoid(SWIGLU_ALPHA * hg) * (hl + 1.0)
        y = jnp.dot(a.astype(BF16), wd_bf[...], preferred_element_type=F32) + bd_ref[0]
        y_ref[...] = y.reshape(y_ref.shape)

    @pl.when(valid_ref[b] == 0)
    def _():
        y_ref[...] = jnp.zeros_like(y_ref)


def _experts(xs, block_e, block_valid, wp):
    cap, d = xs.shape[0], D_MODEL
    n_blocks = cap // EXPERT_BLOCK
    grid_spec = pltpu.PrefetchScalarGridSpec(
        num_scalar_prefetch=2,
        grid=(n_blocks,),
        in_specs=[pl.BlockSpec((EXPERT_BLOCK,) + xs.shape[1:], lambda b, be, bv: (jnp.where(bv[b] == 1, b, 0), 0, 0)),
                  pl.BlockSpec((1, d, 2 * D_FF), lambda b, be, bv: (be[b], 0, 0)),
                  pl.BlockSpec((1, 1, 2 * D_FF), lambda b, be, bv: (be[b], 0, 0)),
                  pl.BlockSpec((1, D_FF, d), lambda b, be, bv: (be[b], 0, 0)),
                  pl.BlockSpec((1, 1, d), lambda b, be, bv: (be[b], 0, 0))],
        out_specs=pl.BlockSpec((EXPERT_BLOCK,) + xs.shape[1:], lambda b, be, bv: (b, 0, 0)),
        scratch_shapes=[pltpu.VMEM((d, 2 * D_FF), BF16), pltpu.VMEM((D_FF, d), BF16)],
    )
    return pl.pallas_call(
        _expert_kernel, grid_spec=grid_spec,
        out_shape=jax.ShapeDtypeStruct(xs.shape, F32),
        compiler_params=_cparams("arbitrary"), name="experts",
    )(block_e, block_valid, xs, wp["wgu"], wp["bgu"], wp["wd"], wp["bd"])


def _combine_kernel(dest_ref, dest_next_ref, x1_ref, gate_ref, mod_ref, gfin_ref, y_hbm, o_ref, buf, sem, *, tm):
    i = pl.program_id(0)
    slot = i % 2

    def gather(dref, s):
        def issue(r, c):
            for k in range(TOP_K):
                d = dref[0, 0, r * TOP_K + k]
                _row_copy(y_hbm.at[pl.ds(d, 1)], buf.at[s, k, pl.ds(r, 1)], sem.at[s]).start(priority=k % 2)
            return c

        lax.fori_loop(0, tm, issue, 0, unroll=ROW_LOOP_UNROLL)

    @pl.when(i == 0)
    def _():
        gather(dest_ref, slot)

    @pl.when(i + 1 < pl.num_programs(0))
    def _():
        gather(dest_next_ref, 1 - slot)

    def drain(r, c):
        for k in range(TOP_K):
            _row_copy(y_hbm.at[pl.ds(0, 1)], buf.at[slot, k, pl.ds(0, 1)], sem.at[slot]).wait()
        return c

    lax.fori_loop(0, tm, drain, 0, unroll=ROW_LOOP_UNROLL)
    gate = gate_ref[...]
    rows = lambda k: buf[slot, k].reshape(tm, D_MODEL)
    y = gate[:, 0:1] * rows(0)
    for k in range(1, TOP_K):
        y = y + gate[:, k:k + 1] * rows(k)
    x2 = x1_ref[...] + mod_ref[_GT_M:_GT_M + 1, :] * y
    o_ref[...] = _rms(x2, gfin_ref[...])


def _combine(x1, gate, mod, g_final, yb, dest, seq, tm):
    n_tok, d = x1.shape
    n_tiles = n_tok // tm
    tiles_per_seq = seq // tm
    dest3 = dest.reshape(n_tiles, 1, tm * TOP_K)
    return pl.pallas_call(
        functools.partial(_combine_kernel, tm=tm),
        grid=(n_tiles,),
        in_specs=[pl.BlockSpec((1, 1, tm * TOP_K), lambda i: (i, 0, 0), memory_space=pltpu.SMEM),
                  pl.BlockSpec((1, 1, tm * TOP_K), lambda i: (jnp.minimum(i + 1, n_tiles - 1), 0, 0),
                               memory_space=pltpu.SMEM),
                  pl.BlockSpec((tm, d), lambda i: (i, 0)),
                  pl.BlockSpec((tm, LANE), lambda i: (i, 0)),
                  pl.BlockSpec((None, MOD_ROWS, d), lambda i: (i // tiles_per_seq, 0, 0)),
                  pl.BlockSpec((1, d), lambda i: (0, 0)),
                  pl.BlockSpec(memory_space=pl.ANY)],
        out_specs=pl.BlockSpec((tm, d), lambda i: (i, 0)),
        out_shape=jax.ShapeDtypeStruct((n_tok, d), F32),
        scratch_shapes=[pltpu.VMEM((2, TOP_K, tm) + yb.shape[1:], F32), pltpu.SemaphoreType.DMA((2,))],
        compiler_params=_cparams("arbitrary"), name="combine",
    )(dest3, dest3, x1, gate, mod, g_final, yb)


def _slot_cols(w, n_heads, width, offset=0):
    k = w.shape[0]
    w3 = w.reshape(k, n_heads, width)
    out = jnp.pad(w3, ((0, 0), (0, 0), (offset, LANE - offset - width)))
    return out.reshape(k, n_heads * LANE)


def _prep_weights(g_attn, g_moe, w_in, g_qn, g_kn, g_qlat, w_uq, g_kvlat, w_ukv, w_br_a, w_br_b, w_out,
                  w_router, b_router, w_gu, b_gu, w_down, b_down):
    d = D_MODEL
    wa = N_HEADS_A * HEAD_DIM_A
    wk = N_KV_A * HEAD_DIM_A
    c0 = 0
    w_qa = w_in[:, c0:c0 + wa]; c0 += wa
    w_ka = w_in[:, c0:c0 + wk]; c0 += wk
    w_va = w_in[:, c0:c0 + wk]; c0 += wk
    w_ql = w_in[:, c0:c0 + Q_LORA]; c0 += Q_LORA
    w_kvl = w_in[:, c0:c0 + KV_LORA]; c0 += KV_LORA
    w_kpe = w_in[:, c0:c0 + ROPE_DIM_B]; c0 += ROPE_DIM_B
    w_gate = w_in[:, c0:]
    w1 = jnp.concatenate([
        _slot_cols(w_qa, N_HEADS_A, HEAD_DIM_A), _slot_cols(w_ka, N_KV_A, HEAD_DIM_A), w_ql, w_kvl,
        _slot_cols(w_kpe, 1, ROPE_DIM_B, NOPE_DIM), w_gate], axis=1).astype(BF16)
    ukv = w_ukv.reshape(KV_LORA, N_HEADS_B, NOPE_DIM + V_DIM_B)
    w_uk = ukv[:, :, :NOPE_DIM].reshape(KV_LORA, N_HEADS_B * NOPE_DIM)
    w_uv = ukv[:, :, NOPE_DIM:].reshape(KV_LORA, N_HEADS_B * V_DIM_B)
    pad_gain = lambda g: jnp.zeros((1, LANE), F32).at[0, :g.shape[0]].set(g)
    return {
        "g_attn": g_attn.reshape(1, d), "g_moe": g_moe.reshape(1, d),
        "w1": w1, "wvat": w_va.T.astype(BF16),
        "g_qa": pad_gain(g_qn), "g_ka": pad_gain(g_kn),
        "g_qlat": g_qlat.reshape(1, Q_LORA), "g_kvlat": g_kvlat.reshape(1, KV_LORA),
        "wuq": _slot_cols(w_uq, N_HEADS_B, NOPE_DIM + ROPE_DIM_B).astype(BF16),
        "wuk": _slot_cols(w_uk, N_HEADS_B, NOPE_DIM).astype(BF16),
        "wuvt": w_uv.T.astype(BF16),
        "wbra": w_br_a.astype(BF16), "wbrb": w_br_b.astype(BF16), "wout": w_out.astype(BF16),
        "wr": jnp.zeros((d, LANE), F32).at[:, :N_EXPERTS].set(w_router).astype(BF16),
        "br": jnp.full((1, LANE), NEG_BIG, F32).at[0, :N_EXPERTS].set(b_router),
        "wgu": w_gu, "bgu": b_gu.reshape(N_EXPERTS, 1, 2 * D_FF),
        "wd": w_down, "bd": b_down.reshape(N_EXPERTS, 1, d),
    }


def _rope_parts(n_rows):
    def three(rot_dim, offset, pos, second_half):
        half = rot_dim // 2
        quarter = half // 2
        inv = ROPE_THETA ** (-np.arange(0, half, 2, dtype=np.float64) / half)
        ang = np.asarray(pos, np.float64)[:, None] * inv
        lo = offset + (half if second_half else 0)
        cos, sinm, sinp = (np.zeros((len(pos), LANE)) for _ in range(3))
        cos[:, lo:lo + quarter] = cos[:, lo + quarter:lo + half] = np.cos(ang)
        sinm[:, lo:lo + quarter] = -np.sin(ang)
        sinp[:, lo + quarter:lo + half] = np.sin(ang)
        if second_half:
            cos[:, :offset] = 1.0
            cos[:, offset + rot_dim:] = 1.0
        return [cos, sinm, sinp]

    def part(pos, second_half):
        tabs = three(HEAD_DIM_A, 0, pos, second_half) + three(ROPE_DIM_B, NOPE_DIM, pos, second_half)
        return jnp.asarray(np.concatenate(tabs, axis=1), F32)

    return part(np.arange(n_rows), False), part(np.arange(GRID_W), True)


def _trunk(x, mod, wp, g_final, rope_row, rope_col):
    n_batch, seq, d = x.shape
    n_tok = n_batch * seq
    x2d = x.reshape(n_tok, d)
    tm = min(256, seq)

    qa, ka, vta, qb, kb, vtb, sig = _pre(x2d, mod, wp, rope_row, rope_col, seq, tm)
    cols = ATTN_COLS
    ota = _attention(qa, ka, vta, n_batch=n_batch, seq=seq, n_groups=N_KV_A, n_sub=GROUP_A,
                     tq=min(cols // GROUP_A, seq), dv=HEAD_DIM_A)
    otb = _attention(qb, kb, vtb, n_batch=n_batch, seq=seq, n_groups=N_HEADS_B, n_sub=1,
                     tq=min(cols, seq), dv=V_DIM_B)
    x1, u2, eid, rank, gate, cnt = _post(x2d, mod, ota, otb, sig, wp, seq, tm)

    n_assign = n_tok * TOP_K
    n_blocks = -(-n_assign // EXPERT_BLOCK) + N_EXPERTS
    cap = n_blocks * EXPERT_BLOCK
    counts = cnt[0, :N_EXPERTS].astype(jnp.int32)
    padded = (counts + EXPERT_BLOCK - 1) // EXPERT_BLOCK * EXPERT_BLOCK
    pad_end = jnp.cumsum(padded)
    pad_start = pad_end - padded
    experts = jnp.arange(N_EXPERTS, dtype=jnp.int32)
    start = jnp.sum(jnp.where(eid[:, :TOP_K, None] == experts, pad_start.astype(jnp.int32), 0), axis=-1)
    dest = start + rank[:, :TOP_K]
    block_lo = jnp.arange(n_blocks, dtype=jnp.int32) * EXPERT_BLOCK
    block_e = jnp.minimum(jnp.sum(block_lo[:, None] >= pad_end[None, :], axis=1), N_EXPERTS - 1).astype(jnp.int32)
    block_valid = (block_lo < pad_end[-1]).astype(jnp.int32)

    xs = _dispatch(u2, dest, pad_end.astype(jnp.int32), cap, min(DISPATCH_TILE, n_tok))
    yb = _experts(xs, block_e, block_valid, wp)
    out = _combine(x1, gate, mod, g_final.reshape(1, d), yb, dest, seq, tm)
    return out.reshape(n_batch, seq, d)


def kernel(x_prompt, x_sample, c_prompt, c_sample, w_mod, b_mod, g_attn, g_moe, w_in, g_qn, g_kn, g_qlat,
           w_uq, g_kvlat, w_ukv, w_br_a, w_br_b, w_out, w_router, b_router, w_gu, b_gu, w_down, b_down,
           g_final):
    assert w_mod.shape[0] == 1, "single-layer trunk"
    wp = _prep_weights(g_attn[0], g_moe[0], w_in[0], g_qn[0], g_kn[0], g_qlat[0], w_uq[0], g_kvlat[0],
                       w_ukv[0], w_br_a[0], w_br_b[0], w_out[0], w_router[0], b_router[0], w_gu[0],
                       b_gu[0], w_down[0], b_down[0])
    nb_p = c_prompt.shape[0]
    mod = _modulation(jnp.concatenate([c_prompt, c_sample], axis=0), w_mod[0], b_mod[0])
    max_seq = max(x_prompt.shape[1], x_sample.shape[1])
    rope_row, rope_col = _rope_parts(max_seq // GRID_W)
    y_prompt = _trunk(x_prompt, mod[:nb_p], wp, g_final, rope_row, rope_col)
    y_sample = _trunk(x_sample, mod[nb_p:], wp, g_final, rope_row, rope_col)
    return (y_prompt, y_sample)
```

```python
import functools

import numpy as np
import jax
import jax.numpy as jnp
from jax import lax
from jax.experimental import pallas as pl
from jax.experimental.pallas import tpu as pltpu

F32 = jnp.float32
BF16 = jnp.bfloat16

D_MODEL = 1024
GRID_W = 64
ROPE_THETA = 10000.0
EPS = 1e-6
N_HEADS_A = 8
N_KV_A = 2
GROUP_A = N_HEADS_A // N_KV_A
HEAD_DIM_A = 64
N_HEADS_B = 8
Q_LORA = 768
KV_LORA = 256
NOPE_DIM = 64
ROPE_DIM_B = 32
V_DIM_B = 64
N_EXPERTS = 32
TOP_K = 4
D_FF = D_MODEL
SWIGLU_ALPHA = 1.702
SWIGLU_LIMIT = 7.0
EXPERT_BLOCK = 256

LANE = 128
BF16_SUBLANES = 16
NEG_BIG = -1e30
LOG2E = 1.4426950408889634
ONES_ROWS = BF16_SUBLANES
VMEM_LIMIT = 52 * 1024 * 1024
ATTN_LAG = 2
ATTN_RING = 4
ATTN_COLS = 1024
ATTN_UNROLL = 32
DISPATCH_TILE = 512
ROW_LOOP_UNROLL = 4
BASE_LANE = LANE - 1
DEN_MIN, DEN_MAX = 1e-30, 1e30

_C_QA = 0
_C_KA = _C_QA + N_HEADS_A * LANE
_C_QLAT = _C_KA + N_KV_A * LANE
_C_KVLAT = _C_QLAT + Q_LORA
_C_KPE = _C_KVLAT + KV_LORA
_C_GATE = _C_KPE + LANE
_C_END = _C_GATE + 2 * D_MODEL

_SH_A, _SC_A, _GT_A, _SH_M, _SC_M, _GT_M = range(6)
MOD_ROWS = 8

_NT = (((1,), (1,)), ((), ()))
_TN = (((0,), (0,)), ((), ()))


def _cparams(*sem):
    return pltpu.CompilerParams(dimension_semantics=sem, vmem_limit_bytes=VMEM_LIMIT)


def _rms(x, g):
    return x * lax.rsqrt(jnp.mean(x * x, axis=-1, keepdims=True) + EPS) * g


def _mod_kernel(c_ref, w_ref, b_ref, o_ref):
    c = c_ref[...]
    s = c * jax.nn.sigmoid(c)
    o_ref[...] = jnp.dot(s.astype(BF16), w_ref[...].astype(BF16), preferred_element_type=F32) + b_ref[...]


def _modulation(c, w_mod, b_mod):
    nb, d = c.shape
    rows = -(-nb // 8) * 8
    cp = jnp.zeros((rows, d), F32).at[:nb].set(c)
    n_out = w_mod.shape[1]
    tn = 512
    out = pl.pallas_call(
        _mod_kernel,
        grid=(n_out // tn,),
        in_specs=[pl.BlockSpec((rows, d), lambda j: (0, 0)),
                  pl.BlockSpec((d, tn), lambda j: (0, j)),
                  pl.BlockSpec((1, tn), lambda j: (0, j))],
        out_specs=pl.BlockSpec((rows, tn), lambda j: (0, j)),
        out_shape=jax.ShapeDtypeStruct((rows, n_out), F32),
        compiler_params=_cparams("arbitrary"),
        name="mod",
    )(cp, w_mod, b_mod.reshape(1, n_out))
    mod = out[:nb].reshape(nb, 6, d)
    return jnp.concatenate([mod, jnp.zeros((nb, MOD_ROWS - 6, d), F32)], axis=1)


def _rope(x, cos, sinm, sinp, shift):
    return x * cos + pltpu.roll(x, LANE - shift, 1) * sinm + pltpu.roll(x, shift, 1) * sinp


def _pre_kernel(x_ref, mod_ref, gattn_ref, w1_ref, wvat_ref, gqa_ref, gka_ref, gqlat_ref, wuq_ref,
                gkvlat_ref, wuk_ref, wuvt_ref, rrow_ref, rcol_ref,
                qa_ref, ka_ref, vta_ref, qb_ref, kb_ref, vtb_ref, sig_ref):
    x = x_ref[...]
    u = _rms(x, gattn_ref[...]) * (1.0 + mod_ref[_SC_A:_SC_A + 1, :]) + mod_ref[_SH_A:_SH_A + 1, :]
    ub = u.astype(BF16)

    def proj(lo, hi):
        return jnp.dot(ub, w1_ref[:, lo:hi], preferred_element_type=F32)

    rcol = rcol_ref[...]
    tab = jnp.concatenate([rrow_ref[r:r + 1, :] + rcol for r in range(rrow_ref.shape[0])], axis=0)
    cos_a, sinm_a, sinp_a, cos_b, sinm_b, sinp_b = (tab[:, j * LANE:(j + 1) * LANE] for j in range(6))
    shift_a = HEAD_DIM_A // 4
    shift_b = ROPE_DIM_B // 4

    def head_a(xs, g, scale):
        ms = jnp.sum(xs * xs, axis=-1, keepdims=True) * (1.0 / HEAD_DIM_A)
        xn = xs * lax.rsqrt(ms + EPS) * g
        return (_rope(xn, cos_a, sinm_a, sinp_a, shift_a) * scale).astype(BF16)

    qa = proj(_C_QA, _C_KA)
    for h in range(N_HEADS_A):
        qa_ref[:, h * LANE:(h + 1) * LANE] = head_a(qa[:, h * LANE:(h + 1) * LANE], gqa_ref[...],
                                                    HEAD_DIM_A ** -0.5 * LOG2E)
    ka = proj(_C_KA, _C_QLAT)
    base_lane = lax.broadcasted_iota(jnp.int32, (x.shape[0], LANE), 1) == BASE_LANE
    for h in range(N_KV_A):
        kh = head_a(ka[:, h * LANE:(h + 1) * LANE], gka_ref[...], 1.0)
        ka_ref[:, h * LANE:(h + 1) * LANE] = jnp.where(base_lane, 1.0, kh).astype(BF16)
    vta = lax.dot_general(wvat_ref[...], ub, _NT, preferred_element_type=F32)
    vta_ref[0, :, :HEAD_DIM_A, :] = vta.astype(BF16).reshape(N_KV_A, HEAD_DIM_A, vta.shape[-1])
    vta_ref[0, :, HEAD_DIM_A:, :] = jnp.ones((N_KV_A, ONES_ROWS, vta.shape[-1]), BF16)

    qn = _rms(proj(_C_QLAT, _C_KVLAT), gqlat_ref[...]).astype(BF16)
    qb = jnp.dot(qn, wuq_ref[...], preferred_element_type=F32)
    scale_b = (NOPE_DIM + ROPE_DIM_B) ** -0.5 * LOG2E
    for h in range(N_HEADS_B):
        sl = slice(h * LANE, (h + 1) * LANE)
        qb_ref[:, sl] = (_rope(qb[:, sl], cos_b, sinm_b, sinp_b, shift_b) * scale_b).astype(BF16)
    kvn = _rms(proj(_C_KVLAT, _C_KPE), gkvlat_ref[...]).astype(BF16)
    kpe = _rope(proj(_C_KPE, _C_GATE), cos_b, sinm_b, sinp_b, shift_b)
    kb = jnp.dot(kvn, wuk_ref[...], preferred_element_type=F32)
    for h in range(N_HEADS_B):
        sl = slice(h * LANE, (h + 1) * LANE)
        kb_ref[:, sl] = jnp.where(base_lane, 1.0, kb[:, sl] + kpe).astype(BF16)
    vtb = lax.dot_general(wuvt_ref[...], kvn, _NT, preferred_element_type=F32)
    vtb_ref[0, :, :V_DIM_B, :] = vtb.astype(BF16).reshape(N_HEADS_B, V_DIM_B, vtb.shape[-1])
    vtb_ref[0, :, V_DIM_B:, :] = jnp.ones((N_HEADS_B, ONES_ROWS, vtb.shape[-1]), BF16)

    sig_ref[...] = jax.nn.sigmoid(proj(_C_GATE, _C_END)).astype(BF16)


def _pre(x2d, mod, wp, rope_row, rope_col, seq, tm):
    n_tok, d = x2d.shape
    n_tiles = n_tok // tm
    tiles_per_seq = seq // tm
    assert tm % GRID_W == 0, "a token tile covers whole grid rows"
    rows_per_tile = tm // GRID_W
    rope_row = rope_row[:seq // GRID_W].reshape(tiles_per_seq, rows_per_tile, rope_row.shape[-1])
    full = lambda a: pl.BlockSpec(a.shape, lambda i: (0,) * a.ndim)
    tok = lambda cols: pl.BlockSpec((tm, cols), lambda i: (i, 0))
    in_specs = [tok(d),
                pl.BlockSpec((None, MOD_ROWS, d), lambda i: (i // tiles_per_seq, 0, 0)),
                full(wp["g_attn"]), full(wp["w1"]), full(wp["wvat"]), full(wp["g_qa"]), full(wp["g_ka"]),
                full(wp["g_qlat"]), full(wp["wuq"]), full(wp["g_kvlat"]), full(wp["wuk"]), full(wp["wuvt"]),
                pl.BlockSpec((None, rows_per_tile, rope_row.shape[-1]), lambda i: (i % tiles_per_seq, 0, 0)),
                full(rope_col)]
    out_shape = [
        jax.ShapeDtypeStruct((n_tok, N_HEADS_A * LANE), BF16),
        jax.ShapeDtypeStruct((n_tok, N_KV_A * LANE), BF16),
        jax.ShapeDtypeStruct((n_tiles, N_KV_A, HEAD_DIM_A + ONES_ROWS, tm), BF16),
        jax.ShapeDtypeStruct((n_tok, N_HEADS_B * LANE), BF16),
        jax.ShapeDtypeStruct((n_tok, N_HEADS_B * LANE), BF16),
        jax.ShapeDtypeStruct((n_tiles, N_HEADS_B, V_DIM_B + ONES_ROWS, tm), BF16),
        jax.ShapeDtypeStruct((n_tok, 2 * D_MODEL), BF16),
    ]
    out_specs = [tok(N_HEADS_A * LANE), tok(N_KV_A * LANE),
                 pl.BlockSpec((1, N_KV_A, HEAD_DIM_A + ONES_ROWS, tm), lambda i: (i, 0, 0, 0)),
                 tok(N_HEADS_B * LANE), tok(N_HEADS_B * LANE),
                 pl.BlockSpec((1, N_HEADS_B, V_DIM_B + ONES_ROWS, tm), lambda i: (i, 0, 0, 0)),
                 tok(2 * D_MODEL)]
    return pl.pallas_call(
        _pre_kernel, grid=(n_tiles,), in_specs=in_specs, out_specs=out_specs, out_shape=out_shape,
        compiler_params=_cparams("arbitrary"), name="pre",
    )(x2d, mod, wp["g_attn"], wp["w1"], wp["wvat"], wp["g_qa"], wp["g_ka"], wp["g_qlat"], wp["wuq"],
      wp["g_kvlat"], wp["wuk"], wp["wuvt"], rope_row, rope_col)


def _attn_kernel(q_ref, k_ref, vt_ref, o_ref, kmax_ref, *p_bufs, n_sub, tq, tk, n_chunks, dv, unroll):
    lag = ATTN_LAG
    cols = n_sub * tq
    if n_sub == 1:
        q = q_ref[...]
    else:
        q = jnp.concatenate([q_ref[:, j * LANE:(j + 1) * LANE] for j in range(n_sub)], axis=0)

    def key_chunk(i):
        start = i * tk if isinstance(i, int) else pl.multiple_of(i * tk, tk)
        return k_ref[pl.ds(start, tk), :]

    @pl.when(pl.program_id(2) == 0)
    def _():
        def norm_body(c, mx):
            kc = key_chunk(c).astype(F32)
            n2 = jnp.sum(kc * kc, axis=-1, keepdims=True)
            return jnp.maximum(mx, jnp.max(n2, axis=0, keepdims=True))

        kmax2 = lax.fori_loop(0, n_chunks, norm_body, jnp.zeros((1, 1), F32))
        kmax_ref[...] = jnp.broadcast_to(jnp.sqrt(jnp.maximum(kmax2 - 1.0, 0.0)), kmax_ref.shape)

    qt = q.astype(F32).T
    base = jnp.sqrt(jnp.sum(qt * qt, axis=0, keepdims=True)) * kmax_ref[0:1, 0:1]
    row = lax.broadcasted_iota(jnp.int32, qt.shape, 0)
    qt_aug = jnp.where(row == BASE_LANE, -base, qt).astype(BF16)

    def pv(c, p_ref):
        return jnp.dot(vt_ref[c], p_ref[...], preferred_element_type=F32)

    def trip(i, u, acc):
        if not isinstance(i, int) or i >= lag:
            acc = acc + pv(i - lag, p_bufs[(u - lag) % ATTN_RING])
        s = jnp.dot(key_chunk(i), qt_aug, preferred_element_type=F32)
        p_bufs[u % ATTN_RING][...] = jnp.exp2(s).astype(BF16)
        return acc

    def body(j, acc):
        for u in range(unroll):
            acc = trip(j * unroll + u, u, acc)
        return acc

    acc = jnp.zeros((dv + ONES_ROWS, cols), F32)
    for u in range(unroll):
        acc = trip(u, u, acc)
    acc = lax.fori_loop(1, n_chunks // unroll, body, acc)
    for c in range(n_chunks - lag, n_chunks):
        acc = acc + pv(c, p_bufs[c % ATTN_RING])

    def emit(acc):
        o = acc[:dv] / acc[dv:dv + 1]
        for j in range(n_sub):
            o_ref[j * dv:(j + 1) * dv, :] = o[:, j * tq:(j + 1) * tq].astype(BF16)

    emit(acc)
    den = acc[dv:dv + 1]
    safe = jnp.logical_and(den > DEN_MIN, den < DEN_MAX)
    n_unsafe = jnp.sum(jnp.where(safe, 0.0, 1.0))

    @pl.when(n_unsafe > 0.0)
    def _():
        def exact_body(i, carry):
            m, acc = carry
            s = lax.dot_general(key_chunk(i), q, _NT, preferred_element_type=F32)
            m_new = jnp.maximum(m, jnp.max(s, axis=0, keepdims=True))
            p = jnp.exp2(s - m_new).astype(BF16)
            return m_new, jnp.exp2(m - m_new) * acc + jnp.dot(vt_ref[i], p, preferred_element_type=F32)

        init = (jnp.full((1, cols), NEG_BIG, F32), jnp.zeros((dv + ONES_ROWS, cols), F32))
        emit(lax.fori_loop(0, n_chunks, exact_body, init)[1])


def _attention(q, k, vt, *, n_batch, seq, n_groups, n_sub, tq, dv):
    n_tok = q.shape[0]
    tk = vt.shape[-1]
    n_chunks = seq // tk
    q_tiles = seq // tq
    unroll = ATTN_UNROLL
    while n_chunks % unroll:
        unroll //= 2
    assert unroll % ATTN_RING == 0, "buffer slots must be static per trip"
    kern = functools.partial(_attn_kernel, n_sub=n_sub, tq=tq, tk=tk, n_chunks=n_chunks, dv=dv,
                             unroll=unroll)
    cols = n_sub * tq
    return pl.pallas_call(
        kern,
        grid=(n_batch, n_groups, q_tiles),
        in_specs=[pl.BlockSpec((tq, n_sub * LANE), lambda b, g, i: (b * q_tiles + i, g)),
                  pl.BlockSpec((seq, LANE), lambda b, g, i: (b, g)),
                  pl.BlockSpec((n_chunks, None, dv + ONES_ROWS, tk), lambda b, g, i: (b, g, 0, 0))],
        out_specs=pl.BlockSpec((n_sub * dv, tq), lambda b, g, i: (g, b * q_tiles + i)),
        out_shape=jax.ShapeDtypeStruct((n_groups * n_sub * dv, n_tok), BF16),
        scratch_shapes=[pltpu.VMEM((8, LANE), F32)] + [pltpu.VMEM((tk, cols), BF16)] * ATTN_RING,
        compiler_params=_cparams("arbitrary", "arbitrary", "arbitrary"),
        name="attn",
    )(q, k, vt)


def _post_kernel(x_ref, mod_ref, ota_ref, otb_ref, sig_ref, wbra_ref, wbrb_ref, wout_ref, gmoe_ref,
                 wr_ref, br_ref,
                 x1_ref, u2_ref, eid_ref, rank_ref, gate_ref, cnt_ref, carry_ref):
    i = pl.program_id(0)

    @pl.when(i == 0)
    def _():
        carry_ref[...] = jnp.zeros_like(carry_ref)

    d = D_MODEL
    ya = lax.dot_general(ota_ref[...], wbra_ref[...], _TN, preferred_element_type=F32)
    yb = lax.dot_general(otb_ref[...], wbrb_ref[...], _TN, preferred_element_type=F32)
    mixed = sig_ref[:, :d].astype(F32) * ya + sig_ref[:, d:].astype(F32) * yb
    att = jnp.dot(mixed.astype(BF16), wout_ref[...], preferred_element_type=F32)
    x1 = x_ref[...] + mod_ref[_GT_A:_GT_A + 1, :] * att
    x1_ref[...] = x1
    u2 = _rms(x1, gmoe_ref[...]) * (1.0 + mod_ref[_SC_M:_SC_M + 1, :]) + mod_ref[_SH_M:_SH_M + 1, :]
    u2_ref[...] = u2.reshape(u2_ref.shape)

    logits = jnp.dot(u2.astype(BF16), wr_ref[...], preferred_element_type=F32) + br_ref[...]
    tm = logits.shape[0]
    lane = lax.broadcasted_iota(jnp.int32, (tm, LANE), 1)
    vals = logits
    top_v, top_i = [], []
    for _ in range(TOP_K):
        mx = jnp.max(vals, axis=-1, keepdims=True)
        idx = jnp.min(jnp.where(vals == mx, lane, LANE), axis=-1, keepdims=True)
        top_v.append(mx)
        top_i.append(idx)
        vals = jnp.where(lane == idx, -jnp.inf, vals)
    ex = [jnp.exp(v - top_v[0]) for v in top_v]
    den = ex[0] + ex[1] + ex[2] + ex[3]

    onehot = [lane == idx for idx in top_i]
    cnt = sum(oh.astype(F32) for oh in onehot)
    r_i = lax.broadcasted_iota(jnp.int32, (tm, tm), 0)
    c_i = lax.broadcasted_iota(jnp.int32, (tm, tm), 1)
    lower = (c_i < r_i).astype(BF16)
    before = jnp.dot(lower, cnt.astype(BF16), preferred_element_type=F32) + carry_ref[...]
    eid = jnp.zeros((tm, LANE), jnp.int32)
    rank = jnp.zeros((tm, LANE), jnp.int32)
    gate = jnp.zeros((tm, LANE), F32)
    for k in range(TOP_K):
        rk = jnp.sum(jnp.where(onehot[k], before, 0.0), axis=-1, keepdims=True).astype(jnp.int32)
        eid = jnp.where(lane == k, top_i[k], eid)
        rank = jnp.where(lane == k, rk, rank)
        gate = jnp.where(lane == k, ex[k] / den, gate)
    eid_ref[...] = eid
    rank_ref[...] = rank
    gate_ref[...] = gate
    carry = carry_ref[...] + jnp.sum(cnt, axis=0, keepdims=True)
    carry_ref[...] = carry
    cnt_ref[...] = carry


def _post(x2d, mod, ota, otb, sig, wp, seq, tm):
    n_tok, d = x2d.shape
    n_tiles = n_tok // tm
    tiles_per_seq = seq // tm
    full = lambda a: pl.BlockSpec(a.shape, lambda i: (0,) * a.ndim)
    tok = lambda cols: pl.BlockSpec((tm, cols), lambda i: (i, 0))
    tcol = lambda rows: pl.BlockSpec((rows, tm), lambda i: (0, i))
    in_specs = [tok(d), pl.BlockSpec((None, MOD_ROWS, d), lambda i: (i // tiles_per_seq, 0, 0)),
                tcol(ota.shape[0]), tcol(otb.shape[0]), tok(2 * d),
                full(wp["wbra"]), full(wp["wbrb"]), full(wp["wout"]), full(wp["g_moe"]),
                full(wp["wr"]), full(wp["br"])]
    out_shape = [jax.ShapeDtypeStruct((n_tok, d), F32), jax.ShapeDtypeStruct((n_tok, d // LANE, LANE), F32),
                 jax.ShapeDtypeStruct((n_tok, LANE), jnp.int32), jax.ShapeDtypeStruct((n_tok, LANE), jnp.int32),
                 jax.ShapeDtypeStruct((n_tok, LANE), F32), jax.ShapeDtypeStruct((1, LANE), F32)]
    out_specs = [tok(d), pl.BlockSpec((tm, d // LANE, LANE), lambda i: (i, 0, 0)),
                 tok(LANE), tok(LANE), tok(LANE), pl.BlockSpec((1, LANE), lambda i: (0, 0))]
    return pl.pallas_call(
        _post_kernel, grid=(n_tiles,), in_specs=in_specs, out_specs=out_specs, out_shape=out_shape,
        scratch_shapes=[pltpu.VMEM((1, LANE), F32)],
        compiler_params=_cparams("arbitrary"), name="post",
    )(x2d, mod, ota, otb, sig, wp["wbra"], wp["wbrb"], wp["wout"], wp["g_moe"], wp["wr"], wp["br"])


def _row_copy(src, dst, sem):
    return pltpu.make_async_copy(src, dst, sem)


def _dispatch_kernel(pad_end_ref, dest_ref, u_ref, xs_ref, zero_ref, sem, zero_sem, *, tm):
    @pl.when(pl.program_id(0) == 0)
    def _():
        zero_ref[...] = jnp.zeros_like(zero_ref)

        def tail_copy(e):
            start = pl.multiple_of(pad_end_ref[e] - EXPERT_BLOCK, EXPERT_BLOCK)
            return pltpu.make_async_copy(zero_ref, xs_ref.at[pl.ds(start, EXPERT_BLOCK)], zero_sem)

        def nonempty(e):
            return pad_end_ref[e] > (pad_end_ref[e - 1] if e else 0)

        for e in range(N_EXPERTS):
            pl.when(nonempty(e))(lambda e=e: tail_copy(e).start())
        for e in range(N_EXPERTS):
            pl.when(nonempty(e))(lambda e=e: tail_copy(e).wait())

        def unused_copy(b):
            start = pl.multiple_of(b * EXPERT_BLOCK, EXPERT_BLOCK)
            return pltpu.make_async_copy(zero_ref, xs_ref.at[pl.ds(start, EXPERT_BLOCK)], zero_sem)

        first_unused = pad_end_ref[N_EXPERTS - 1] // EXPERT_BLOCK
        n_blocks = xs_ref.shape[0] // EXPERT_BLOCK
        lax.fori_loop(first_unused, n_blocks, lambda b, c: (unused_copy(b).start(), c)[1], 0)
        lax.fori_loop(first_unused, n_blocks, lambda b, c: (unused_copy(b).wait(), c)[1], 0)

    def issue(r, c):
        for k in range(TOP_K):
            d = dest_ref[0, 0, r * TOP_K + k]
            _row_copy(u_ref.at[pl.ds(r, 1)], xs_ref.at[pl.ds(d, 1)], sem).start(priority=k % 2)
        return c

    lax.fori_loop(0, tm, issue, 0, unroll=ROW_LOOP_UNROLL)

    def drain(r, c):
        for k in range(TOP_K):
            _row_copy(u_ref.at[pl.ds(0, 1)], xs_ref.at[pl.ds(0, 1)], sem).wait()
        return c

    lax.fori_loop(0, tm, drain, 0, unroll=ROW_LOOP_UNROLL)


def _dispatch(u2, dest, pad_end, cap, tm):
    n_tok, row = u2.shape[0], u2.shape[1:]
    n_tiles = n_tok // tm
    dest3 = dest.reshape(n_tiles, 1, tm * TOP_K)
    grid_spec = pltpu.PrefetchScalarGridSpec(
        num_scalar_prefetch=1,
        grid=(n_tiles,),
        in_specs=[pl.BlockSpec((1, 1, tm * TOP_K), lambda i, pe: (i, 0, 0), memory_space=pltpu.SMEM),
                  pl.BlockSpec((tm,) + row, lambda i, pe: (i, 0, 0))],
        out_specs=pl.BlockSpec(memory_space=pl.ANY),
        scratch_shapes=[pltpu.VMEM((EXPERT_BLOCK,) + row, u2.dtype), pltpu.SemaphoreType.DMA(()),
                        pltpu.SemaphoreType.DMA(())],
    )
    return pl.pallas_call(
        functools.partial(_dispatch_kernel, tm=tm), grid_spec=grid_spec,
        out_shape=jax.ShapeDtypeStruct((cap,) + row, u2.dtype),
        compiler_params=_cparams("arbitrary"), name="dispatch",
    )(pad_end, dest3, u2)


def _expert_kernel(be_ref, valid_ref, run_ref, next_ref, xs_ref, wgu_hbm, bgu_ref, wd_hbm, bd_ref, y_ref,
                   wgu_f32, wd_f32, wgu_bf, wd_bf, sem):
    b = pl.program_id(0)
    slot = run_ref[b] % 2

    def weight_copies(e, s):
        return (pltpu.make_async_copy(wgu_hbm.at[e], wgu_f32.at[s], sem.at[0, s]),
                pltpu.make_async_copy(wd_hbm.at[e], wd_f32.at[s], sem.at[1, s]))

    @pl.when(b == 0)
    def _():
        for c in weight_copies(be_ref[0], 0):
            c.start()

    @pl.when(jnp.logical_or(b == 0, be_ref[b] != be_ref[jnp.maximum(b - 1, 0)]))
    def _():
        for c in weight_copies(be_ref[b], slot):
            c.wait()
        wgu_bf[...] = wgu_f32[slot].astype(BF16)
        wd_bf[...] = wd_f32[slot].astype(BF16)

        @pl.when(next_ref[b] >= 0)
        def _():
            for c in weight_copies(next_ref[b], 1 - slot):
                c.start()

    @pl.when(valid_ref[b] == 1)
    def _():
        x = xs_ref[...].reshape(EXPERT_BLOCK, D_MODEL).astype(BF16)
        h = jnp.dot(x, wgu_bf[...], preferred_element_type=F32) + bgu_ref[0]
        hg = jnp.minimum(h[:, :D_FF], SWIGLU_LIMIT)
        hl = jnp.clip(h[:, D_FF:], -SWIGLU_LIMIT, SWIGLU_LIMIT)
        a = hg * jax.nn.sigmoid(SWIGLU_ALPHA * hg) * (hl + 1.0)
        y = jnp.dot(a.astype(BF16), wd_bf[...], preferred_element_type=F32) + bd_ref[0]
        y_ref[...] = y.reshape(y_ref.shape)

    @pl.when(valid_ref[b] == 0)
    def _():
        y_ref[...] = jnp.zeros_like(y_ref)


def _experts(xs, block_e, block_valid, wp):
    cap, d = xs.shape[0], D_MODEL
    n_blocks = cap // EXPERT_BLOCK
    first = jnp.concatenate([jnp.ones((1,), bool), block_e[1:] != block_e[:-1]])
    run = (jnp.cumsum(first) - 1).astype(jnp.int32)
    idx = jnp.arange(n_blocks, dtype=jnp.int32)
    next_first = jnp.min(jnp.where(first[None, :] & (idx[None, :] > idx[:, None]), idx[None, :], n_blocks), axis=1)
    next_e = jnp.where(next_first < n_blocks, block_e[jnp.minimum(next_first, n_blocks - 1)], -1).astype(jnp.int32)
    grid_spec = pltpu.PrefetchScalarGridSpec(
        num_scalar_prefetch=4,
        grid=(n_blocks,),
        in_specs=[pl.BlockSpec((EXPERT_BLOCK,) + xs.shape[1:], lambda b, be, bv, rn, nx: (jnp.where(bv[b] == 1, b, 0), 0, 0)),
                  pl.BlockSpec(memory_space=pl.ANY),
                  pl.BlockSpec((1, 1, 2 * D_FF), lambda b, be, bv, rn, nx: (be[b], 0, 0)),
                  pl.BlockSpec(memory_space=pl.ANY),
                  pl.BlockSpec((1, 1, d), lambda b, be, bv, rn, nx: (be[b], 0, 0))],
        out_specs=pl.BlockSpec((EXPERT_BLOCK,) + xs.shape[1:], lambda b, be, bv, rn, nx: (b, 0, 0)),
        scratch_shapes=[pltpu.VMEM((2, d, 2 * D_FF), F32), pltpu.VMEM((2, D_FF, d), F32),
                        pltpu.VMEM((d, 2 * D_FF), BF16), pltpu.VMEM((D_FF, d), BF16),
                        pltpu.SemaphoreType.DMA((2, 2))],
    )
    return pl.pallas_call(
        _expert_kernel, grid_spec=grid_spec,
        out_shape=jax.ShapeDtypeStruct(xs.shape, F32),
        compiler_params=_cparams("arbitrary"), name="experts",
    )(block_e, block_valid, run, next_e, xs, wp["wgu"], wp["bgu"], wp["wd"], wp["bd"])


def _combine_kernel(dest_ref, dest_next_ref, x1_ref, gate_ref, mod_ref, gfin_ref, y_hbm, o_ref, buf, sem, *, tm):
    i = pl.program_id(0)
    slot = i % 2

    def gather(dref, s):
        def issue(r, c):
            for k in range(TOP_K):
                d = dref[0, 0, r * TOP_K + k]
                _row_copy(y_hbm.at[pl.ds(d, 1)], buf.at[s, k, pl.ds(r, 1)], sem.at[s]).start(priority=k % 2)
            return c

        lax.fori_loop(0, tm, issue, 0, unroll=ROW_LOOP_UNROLL)

    @pl.when(i == 0)
    def _():
        gather(dest_ref, slot)

    @pl.when(i + 1 < pl.num_programs(0))
    def _():
        gather(dest_next_ref, 1 - slot)

    def drain(r, c):
        for k in range(TOP_K):
            _row_copy(y_hbm.at[pl.ds(0, 1)], buf.at[slot, k, pl.ds(0, 1)], sem.at[slot]).wait()
        return c

    lax.fori_loop(0, tm, drain, 0, unroll=ROW_LOOP_UNROLL)
    gate = gate_ref[...]
    rows = lambda k: buf[slot, k].reshape(tm, D_MODEL)
    y = gate[:, 0:1] * rows(0)
    for k in range(1, TOP_K):
        y = y + gate[:, k:k + 1] * rows(k)
    x2 = x1_ref[...] + mod_ref[_GT_M:_GT_M + 1, :] * y
    o_ref[...] = _rms(x2, gfin_ref[...])


def _combine(x1, gate, mod, g_final, yb, dest, seq, tm):
    n_tok, d = x1.shape
    n_tiles = n_tok // tm
    tiles_per_seq = seq // tm
    dest3 = dest.reshape(n_tiles, 1, tm * TOP_K)
    return pl.pallas_call(
        functools.partial(_combine_kernel, tm=tm),
        grid=(n_tiles,),
        in_specs=[pl.BlockSpec((1, 1, tm * TOP_K), lambda i: (i, 0, 0), memory_space=pltpu.SMEM),
                  pl.BlockSpec((1, 1, tm * TOP_K), lambda i: (jnp.minimum(i + 1, n_tiles - 1), 0, 0),
                               memory_space=pltpu.SMEM),
                  pl.BlockSpec((tm, d), lambda i: (i, 0)),
                  pl.BlockSpec((tm, LANE), lambda i: (i, 0)),
                  pl.BlockSpec((None, MOD_ROWS, d), lambda i: (i // tiles_per_seq, 0, 0)),
                  pl.BlockSpec((1, d), lambda i: (0, 0)),
                  pl.BlockSpec(memory_space=pl.ANY)],
        out_specs=pl.BlockSpec((tm, d), lambda i: (i, 0)),
        out_shape=jax.ShapeDtypeStruct((n_tok, d), F32),
        scratch_shapes=[pltpu.VMEM((2, TOP_K, tm) + yb.shape[1:], F32), pltpu.SemaphoreType.DMA((2,))],
        compiler_params=_cparams("arbitrary"), name="combine",
    )(dest3, dest3, x1, gate, mod, g_final, yb)


def _slot_cols(w, n_heads, width, offset=0):
    k = w.shape[0]
    w3 = w.reshape(k, n_heads, width)
    out = jnp.pad(w3, ((0, 0), (0, 0), (offset, LANE - offset - width)))
    return out.reshape(k, n_heads * LANE)


def _prep_weights(g_attn, g_moe, w_in, g_qn, g_kn, g_qlat, w_uq, g_kvlat, w_ukv, w_br_a, w_br_b, w_out,
                  w_router, b_router, w_gu, b_gu, w_down, b_down):
    d = D_MODEL
    wa = N_HEADS_A * HEAD_DIM_A
    wk = N_KV_A * HEAD_DIM_A
    c0 = 0
    w_qa = w_in[:, c0:c0 + wa]; c0 += wa
    w_ka = w_in[:, c0:c0 + wk]; c0 += wk
    w_va = w_in[:, c0:c0 + wk]; c0 += wk
    w_ql = w_in[:, c0:c0 + Q_LORA]; c0 += Q_LORA
    w_kvl = w_in[:, c0:c0 + KV_LORA]; c0 += KV_LORA
    w_kpe = w_in[:, c0:c0 + ROPE_DIM_B]; c0 += ROPE_DIM_B
    w_gate = w_in[:, c0:]
    w1 = jnp.concatenate([
        _slot_cols(w_qa, N_HEADS_A, HEAD_DIM_A), _slot_cols(w_ka, N_KV_A, HEAD_DIM_A), w_ql, w_kvl,
        _slot_cols(w_kpe, 1, ROPE_DIM_B, NOPE_DIM), w_gate], axis=1).astype(BF16)
    ukv = w_ukv.reshape(KV_LORA, N_HEADS_B, NOPE_DIM + V_DIM_B)
    w_uk = ukv[:, :, :NOPE_DIM].reshape(KV_LORA, N_HEADS_B * NOPE_DIM)
    w_uv = ukv[:, :, NOPE_DIM:].reshape(KV_LORA, N_HEADS_B * V_DIM_B)
    pad_gain = lambda g: jnp.zeros((1, LANE), F32).at[0, :g.shape[0]].set(g)
    return {
        "g_attn": g_attn.reshape(1, d), "g_moe": g_moe.reshape(1, d),
        "w1": w1, "wvat": w_va.T.astype(BF16),
        "g_qa": pad_gain(g_qn), "g_ka": pad_gain(g_kn),
        "g_qlat": g_qlat.reshape(1, Q_LORA), "g_kvlat": g_kvlat.reshape(1, KV_LORA),
        "wuq": _slot_cols(w_uq, N_HEADS_B, NOPE_DIM + ROPE_DIM_B).astype(BF16),
        "wuk": _slot_cols(w_uk, N_HEADS_B, NOPE_DIM).astype(BF16),
        "wuvt": w_uv.T.astype(BF16),
        "wbra": w_br_a.astype(BF16), "wbrb": w_br_b.astype(BF16), "wout": w_out.astype(BF16),
        "wr": jnp.zeros((d, LANE), F32).at[:, :N_EXPERTS].set(w_router).astype(BF16),
        "br": jnp.full((1, LANE), NEG_BIG, F32).at[0, :N_EXPERTS].set(b_router),
        "wgu": w_gu, "bgu": b_gu.reshape(N_EXPERTS, 1, 2 * D_FF),
        "wd": w_down, "bd": b_down.reshape(N_EXPERTS, 1, d),
    }


def _rope_parts(n_rows):
    def three(rot_dim, offset, pos, second_half):
        half = rot_dim // 2
        quarter = half // 2
        inv = ROPE_THETA ** (-np.arange(0, half, 2, dtype=np.float64) / half)
        ang = np.asarray(pos, np.float64)[:, None] * inv
        lo = offset + (half if second_half else 0)
        cos, sinm, sinp = (np.zeros((len(pos), LANE)) for _ in range(3))
        cos[:, lo:lo + quarter] = cos[:, lo + quarter:lo + half] = np.cos(ang)
        sinm[:, lo:lo + quarter] = -np.sin(ang)
        sinp[:, lo + quarter:lo + half] = np.sin(ang)
        if second_half:
            cos[:, :offset] = 1.0
            cos[:, offset + rot_dim:] = 1.0
        return [cos, sinm, sinp]

    def part(pos, second_half):
        tabs = three(HEAD_DIM_A, 0, pos, second_half) + three(ROPE_DIM_B, NOPE_DIM, pos, second_half)
        return jnp.asarray(np.concatenate(tabs, axis=1), F32)

    return part(np.arange(n_rows), False), part(np.arange(GRID_W), True)


def _trunk(x, mod, wp, g_final, rope_row, rope_col):
    n_batch, seq, d = x.shape
    n_tok = n_batch * seq
    x2d = x.reshape(n_tok, d)
    tm = min(256, seq)

    qa, ka, vta, qb, kb, vtb, sig = _pre(x2d, mod, wp, rope_row, rope_col, seq, tm)
    cols = ATTN_COLS
    ota = _attention(qa, ka, vta, n_batch=n_batch, seq=seq, n_groups=N_KV_A, n_sub=GROUP_A,
                     tq=min(cols // GROUP_A, seq), dv=HEAD_DIM_A)
    otb = _attention(qb, kb, vtb, n_batch=n_batch, seq=seq, n_groups=N_HEADS_B, n_sub=1,
                     tq=min(cols, seq), dv=V_DIM_B)
    x1, u2, eid, rank, gate, cnt = _post(x2d, mod, ota, otb, sig, wp, seq, tm)

    n_assign = n_tok * TOP_K
    n_blocks = -(-n_assign // EXPERT_BLOCK) + N_EXPERTS
    cap = n_blocks * EXPERT_BLOCK
    counts = cnt[0, :N_EXPERTS].astype(jnp.int32)
    padded = (counts + EXPERT_BLOCK - 1) // EXPERT_BLOCK * EXPERT_BLOCK
    pad_end = jnp.cumsum(padded)
    pad_start = pad_end - padded
    experts = jnp.arange(N_EXPERTS, dtype=jnp.int32)
    start = jnp.sum(jnp.where(eid[:, :TOP_K, None] == experts, pad_start.astype(jnp.int32), 0), axis=-1)
    dest = start + rank[:, :TOP_K]
    block_lo = jnp.arange(n_blocks, dtype=jnp.int32) * EXPERT_BLOCK
    block_e = jnp.minimum(jnp.sum(block_lo[:, None] >= pad_end[None, :], axis=1), N_EXPERTS - 1).astype(jnp.int32)
    block_valid = (block_lo < pad_end[-1]).astype(jnp.int32)

    xs = _dispatch(u2, dest, pad_end.astype(jnp.int32), cap, min(DISPATCH_TILE, n_tok))
    yb = _experts(xs, block_e, block_valid, wp)
    out = _combine(x1, gate, mod, g_final.reshape(1, d), yb, dest, seq, tm)
    return out.reshape(n_batch, seq, d)


def kernel(x_prompt, x_sample, c_prompt, c_sample, w_mod, b_mod, g_attn, g_moe, w_in, g_qn, g_kn, g_qlat,
           w_uq, g_kvlat, w_ukv, w_br_a, w_br_b, w_out, w_router, b_router, w_gu, b_gu, w_down, b_down,
           g_final):
    assert w_mod.shape[0] == 1, "single-layer trunk"
    wp = _prep_weights(g_attn[0], g_moe[0], w_in[0], g_qn[0], g_kn[0], g_qlat[0], w_uq[0], g_kvlat[0],
                       w_ukv[0], w_br_a[0], w_br_b[0], w_out[0], w_router[0], b_router[0], w_gu[0],
                       b_gu[0], w_down[0], b_down[0])
    nb_p = c_prompt.shape[0]
    mod = _modulation(jnp.concatenate([c_prompt, c_sample], axis=0), w_mod[0], b_mod[0])
    max_seq = max(x_prompt.shape[1], x_sample.shape[1])
    rope_row, rope_col = _rope_parts(max_seq // GRID_W)
    y_prompt = _trunk(x_prompt, mod[:nb_p], wp, g_final, rope_row, rope_col)
    y_sample = _trunk(x_sample, mod[nb_p:], wp, g_final, rope_row, rope_col)
    return (y_prompt, y_sample)
```
